```python
import jax, jax.numpy as jnp
from jax import lax
import numpy as np

D_MODEL = 1024
BATCH = 8
SEQ = 4096
DEPTH = 2

GLA_HEADS = 4
GLA_DK = 32
GLA_DV = 64
GLA_RANK = 16
GLA_GATE_NORM = 16.0
GLA_CHUNK = 64
RET_HEADS = 4
RET_DK = 64
RET_DV = 64
RET_CHUNK = 128
ROPE_BASE = 10000.0
SB_HEADS = 8
SB_DK = 64
SB_DV = 64
SB_BLOCK = 128
D_MIX = GLA_HEADS * GLA_DV + RET_HEADS * RET_DV + SB_HEADS * SB_DV
D_FF = -(-8 * D_MODEL // (3 * 256)) * 256
ADA_SCALE = 0.5
EPS = 1e-6

IN_SIZES = (GLA_HEADS * GLA_DK, GLA_HEADS * GLA_DK, GLA_HEADS * GLA_DV, GLA_HEADS * GLA_DV, GLA_RANK,
            RET_HEADS * RET_DK, RET_HEADS * RET_DK, RET_HEADS * RET_DV, RET_HEADS * RET_DV,
            SB_HEADS * SB_DK, SB_HEADS * SB_DK, SB_HEADS * SB_DV)
IN_COLS = sum(IN_SIZES)
SPLIT_POINTS = tuple(int(s) for s in np.cumsum(IN_SIZES)[:-1])

kernel_name = "hybrid_gla_retnet_stickbreak_adaln"


def rms_norm(x, g):
    xf = x.astype(jnp.float32)
    y = xf * lax.rsqrt(jnp.mean(xf * xf, axis=-1, keepdims=True) + EPS)
    return (y * g.astype(jnp.float32)).astype(x.dtype)


def gla_mixer(q, k, v, log_a):
    B, S, H, dk = q.shape
    dv = v.shape[-1]
    C = GLA_CHUNK
    N = S // C

    def to_chunks(t):
        return t.reshape(B, N, C, H, t.shape[-1]).transpose(1, 0, 3, 2, 4)

    qc, kc, vc, ac = (to_chunks(t) for t in (q * dk ** -0.5, k, v, log_a))
    causal = jnp.tril(jnp.ones((C, C), dtype=bool))[:, :, None]

    def step(state, inp):
        qi, ki, vi, ai = inp
        b = jnp.cumsum(ai, axis=2)
        diff = b[:, :, :, None, :] - b[:, :, None, :, :]
        decay = jnp.exp(jnp.where(causal, diff, -jnp.inf))
        scores = jnp.einsum('bhtd,bhsd,bhtsd->bhts', qi, ki, decay)
        o = (jnp.einsum('bhts,bhsv->bhtv', scores, vi)
             + jnp.einsum('bhtd,bhdv->bhtv', qi * jnp.exp(b), state))
        b_last = b[:, :, -1:, :]
        state = (jnp.exp(b_last[:, :, 0, :, None]) * state
                 + jnp.einsum('bhsd,bhsv->bhdv', ki * jnp.exp(b_last - b), vi))
        return state, o

    state0 = jnp.zeros((B, H, dk, dv), jnp.float32)
    _, o = lax.scan(step, state0, (qc, kc, vc, ac))
    return o.transpose(1, 0, 3, 2, 4).reshape(B, S, H, dv)


def rotary(t, pos):
    d = t.shape[-1]
    inv = ROPE_BASE ** (-jnp.arange(0, d, 2, dtype=jnp.float32) / d)
    ang = pos.astype(jnp.float32)[:, None] * inv[None, :]
    cos = jnp.cos(ang)[None, :, None, :]
    sin = jnp.sin(ang)[None, :, None, :]
    t1, t2 = t[..., 0::2], t[..., 1::2]
    return jnp.stack([t1 * cos - t2 * sin, t1 * sin + t2 * cos], axis=-1).reshape(t.shape)


def retention_mixer(q, k, v):
    B, S, H, dk = q.shape
    dv = v.shape[-1]
    C = RET_CHUNK
    N = S // C
    pos = jnp.arange(S)
    q = rotary(q, pos)
    k = rotary(k, pos) * dk ** -0.5
    log_g = jnp.log(1.0 - 2.0 ** (-5.0 - jnp.arange(H, dtype=jnp.float32)))

    def to_chunks(t):
        return t.reshape(B, N, C, H, t.shape[-1]).transpose(0, 3, 1, 2, 4)

    qc, kc, vc = to_chunks(q), to_chunks(k), to_chunks(v)
    idx = jnp.arange(C, dtype=jnp.float32)
    rel = idx[:, None] - idx[None, :]
    dmat = jnp.where(rel >= 0, jnp.exp(log_g[:, None, None] * jnp.maximum(rel, 0.0)), 0.0)
    scores = jnp.einsum('bhntd,bhnsd->bhnts', qc, kc) * dmat[None, :, None]
    o_intra = jnp.einsum('bhnts,bhnsv->bhntv', scores, vc)

    zeta = jnp.exp(log_g[:, None] * (C - 1.0 - idx)[None, :])
    kv = jnp.einsum('bhnsd,bhnsv->nbhdv', kc * zeta[None, :, None, :, None], vc)
    gamma_c = jnp.exp(log_g * C)[None, :, None, None]

    def step(r, kv_i):
        return gamma_c * r + kv_i, r

    _, r_prev = lax.scan(step, jnp.zeros((B, H, dk, dv), jnp.float32), kv)
    xi = jnp.exp(log_g[:, None] * (idx + 1.0)[None, :])
    o_inter = jnp.einsum('bhntd,nbhdv->bhntv', qc * xi[None, :, None, :, None], r_prev)
    o = o_intra + o_inter
    return o.transpose(0, 2, 3, 1, 4).reshape(B, S, H, dv)


def stick_breaking_mixer(q, k, v):
    B, S, H, d = q.shape
    T = SB_BLOCK
    q = q.transpose(0, 2, 1, 3) * d ** -0.5
    k = k.transpose(0, 2, 1, 3)
    v = v.transpose(0, 2, 1, 3)
    outs = []
    for i in range(S // T):
        end = (i + 1) * T
        qb, kb, vb = q[:, :, i * T:end], k[:, :, :end], v[:, :, :end]
        z = jnp.einsum('bhtd,bhsd->bhts', qb, kb)
        mask = jnp.arange(end)[None, :] < (i * T + jnp.arange(T))[:, None]
        log_1m = jnp.where(mask, jax.nn.log_sigmoid(-z), 0.0)
        log_a = jax.nn.log_sigmoid(z) + lax.cumsum(log_1m, axis=3, reverse=True) - log_1m
        attn = jnp.where(mask, jnp.exp(log_a), 0.0)
        outs.append(jnp.einsum('bhts,bhsv->bhtv', attn, vb))
    o = jnp.concatenate(outs, axis=2)
    return o.transpose(0, 2, 1, 3)


def hybrid_mixer(h, w_in, gla_wa2, gla_ba, gla_norm_g, ret_norm_g, w_out):
    B, S, _ = h.shape
    proj = jnp.einsum('bsd,de->bse', h, w_in).astype(jnp.float32)
    gq, gk, gv, gg, gr, rq, rk, rv, rg, sq, sk, sv = jnp.split(proj, SPLIT_POINTS, axis=-1)

    def heads(t, n):
        return t.reshape(B, S, n, -1)

    log_a = jax.nn.log_sigmoid(gr @ gla_wa2.astype(jnp.float32) + gla_ba.astype(jnp.float32)) / GLA_GATE_NORM
    o_gla = gla_mixer(heads(gq, GLA_HEADS), heads(gk, GLA_HEADS), heads(gv, GLA_HEADS), heads(log_a, GLA_HEADS))
    o_gla = rms_norm(o_gla, gla_norm_g).reshape(B, S, -1) * jax.nn.silu(gg)
    o_ret = retention_mixer(heads(rq, RET_HEADS), heads(rk, RET_HEADS), heads(rv, RET_HEADS))
    o_ret = rms_norm(o_ret, ret_norm_g).reshape(B, S, -1) * jax.nn.silu(rg)
    o_sb = stick_breaking_mixer(heads(sq, SB_HEADS), heads(sk, SB_HEADS), heads(sv, SB_HEADS)).reshape(B, S, -1)

    o = jnp.concatenate([o_gla, o_ret, o_sb], axis=-1).astype(h.dtype)
    return jnp.einsum('bse,ed->bsd', o, w_out)


def swiglu(h, wg, wu, wd):
    a = jnp.einsum('bsd,df->bsf', h, wg)
    b = jnp.einsum('bsd,df->bsf', h, wu)
    return jnp.einsum('bsf,fd->bsd', jax.nn.silu(a) * b, wd)


def setup_inputs(seed: int = 0) -> dict:
    key = jax.random.key(seed)
    ks = jax.random.split(key, 16)
    L, D = DEPTH, D_MODEL

    def nrm(k, shape, fan_in):
        return jax.random.normal(k, shape, jnp.float32) * fan_in ** -0.5

    def small(k, shape):
        return 0.02 * jax.random.normal(k, shape, jnp.float32)

    return {
        'x': jax.random.normal(ks[0], (BATCH, SEQ, D), jnp.float32),
        'c': jax.random.normal(ks[1], (BATCH, D), jnp.float32),
        'ada_w': nrm(ks[2], (L, D, 6 * D), D) * ADA_SCALE,
        'ada_b': small(ks[3], (L, 6 * D)),
        'norm1_g': 1.0 + small(ks[4], (L, D)),
        'norm2_g': 1.0 + small(ks[5], (L, D)),
        'w_in': nrm(ks[6], (L, D, IN_COLS), D),
        'gla_wa2': nrm(ks[7], (L, GLA_RANK, GLA_HEADS * GLA_DK), GLA_RANK),
        'gla_ba': small(ks[8], (L, GLA_HEADS * GLA_DK)),
        'gla_norm_g': 1.0 + small(ks[9], (L, GLA_DV)),
        'ret_norm_g': 1.0 + small(ks[10], (L, RET_DV)),
        'w_out': nrm(ks[11], (L, D_MIX, D), D_MIX),
        'ffn_wg': nrm(ks[12], (L, D, D_FF), D),
        'ffn_wu': nrm(ks[13], (L, D, D_FF), D),
        'ffn_wd': nrm(ks[14], (L, D_FF, D), D_FF),
        'final_g': 1.0 + small(ks[15], (D,)),
    }


def reference(x, c, ada_w, ada_b, norm1_g, norm2_g, w_in, gla_wa2, gla_ba, gla_norm_g,
              ret_norm_g, w_out, ffn_wg, ffn_wu, ffn_wd, final_g):
    cs = jax.nn.silu(c)
    for l in range(DEPTH):
        mod = cs @ ada_w[l] + ada_b[l]
        sh1, sc1, g1, sh2, sc2, g2 = (m[:, None, :] for m in jnp.split(mod, 6, axis=-1))
        h = rms_norm(x, norm1_g[l]) * (1.0 + sc1) + sh1
        x = x + g1 * hybrid_mixer(h, w_in[l], gla_wa2[l], gla_ba[l], gla_norm_g[l], ret_norm_g[l], w_out[l])
        h = rms_norm(x, norm2_g[l]) * (1.0 + sc2) + sh2
        x = x + g2 * swiglu(h, ffn_wg[l], ffn_wu[l], ffn_wd[l])
    return rms_norm(x, final_g)
```

```python
import functools

import jax
import jax.numpy as jnp
import numpy as np
from jax import lax
from jax.experimental import pallas as pl
from jax.experimental.pallas import tpu as pltpu

F32 = jnp.float32
BF16 = jnp.bfloat16

D_MODEL = 1024
DEPTH = 2
GLA_HEADS, GLA_DK, GLA_DV, GLA_RANK = 4, 32, 64, 16
GLA_GATE_NORM = 16.0
RET_HEADS, RET_DK, RET_DV = 4, 64, 64
ROPE_BASE = 10000.0
SB_HEADS, SB_DK, SB_DV = 8, 64, 64
D_FF = 2816
EPS = 1e-6

GLA_QK = GLA_HEADS * GLA_DK
GLA_V = GLA_HEADS * GLA_DV
RET_W = RET_HEADS * RET_DK
SB_W = SB_HEADS * SB_DK
C_GQ, C_GK, C_GV, C_GG = 0, 128, 256, 512
C_RQ, C_RK, C_RV, C_RG = 768, 1024, 1280, 1536
C_SQ, C_SK, C_SV = 1792, 2304, 2816
C_GR = 3328
IN_COLS_PADDED = 3456

LANES = 128
GLA_CHUNK = 64
GLA_LEVELS = 6
GLA_BLOCK = 256
RET_CHUNK = 256
SB_TILE = 256
TOKEN_TILE = 512
FF_CHUNK = 256
VMEM_LIMIT = 56 * 1024 * 1024


def _dot(a, b):
    return jnp.dot(a, b, preferred_element_type=F32)


def _dot_nt(a, b):
    return lax.dot_general(a, b, (((1,), (1,)), ((), ())), preferred_element_type=F32)


def _dot_tn(a, b):
    return lax.dot_general(a, b, (((0,), (0,)), ((), ())), preferred_element_type=F32)


def _sigmoid(x):
    return 1.0 / (1.0 + jnp.exp(-x))


def _softplus(x):
    return jnp.maximum(x, 0.0) + jnp.log(1.0 + jnp.exp(-jnp.abs(x)))


def _params(*sem):
    return pltpu.CompilerParams(dimension_semantics=sem, vmem_limit_bytes=VMEM_LIMIT)


def _mod_kernel(c_ref, w_ref, b_ref, o_ref):
    c = c_ref[...]
    cs = c * _sigmoid(c)
    o_ref[0] = jnp.dot(cs, w_ref[0], preferred_element_type=F32,
                       precision=lax.Precision.HIGHEST) + b_ref[0]


def _modulation(c, ada_w, ada_b):
    depth, d, n = ada_w.shape
    b = c.shape[0]
    tn = 1536
    return pl.pallas_call(
        _mod_kernel,
        grid=(depth, n // tn),
        in_specs=[pl.BlockSpec((b, d), lambda l, j: (0, 0)),
                  pl.BlockSpec((1, d, tn), lambda l, j: (l, 0, j)),
                  pl.BlockSpec((1, 1, tn), lambda l, j: (l, 0, j))],
        out_specs=pl.BlockSpec((1, b, tn), lambda l, j: (l, 0, j)),
        out_shape=jax.ShapeDtypeStruct((depth, b, n), F32),
        compiler_params=_params("arbitrary", "arbitrary"),
        name="adaln_modulation",
    )(c, ada_w, ada_b.reshape(depth, 1, n))


def _rope(t, cos, sin):
    lane = lax.broadcasted_iota(jnp.int32, cos.shape, 1)
    even = (lane & 1) == 0
    outs = []
    for c in range(t.shape[1] // LANES):
        tc = t[:, c * LANES:(c + 1) * LANES]
        nxt = pltpu.roll(tc, LANES - 1, 1)
        prv = pltpu.roll(tc, 1, 1)
        outs.append(tc * cos + jnp.where(even, nxt, prv) * sin)
    return jnp.concatenate(outs, axis=1)


def _inproj_kernel(x_ref, mod_ref, g_ref, w_ref, wa2_ref, ba_ref, cos_ref, sin_ref,
                   gq_ref, gk_ref, gv_ref, gg_ref, la_ref,
                   rq_ref, rk_ref, rv_ref, rg_ref, sqT_ref, sk_ref, svT_ref):
    d = D_MODEL
    x = x_ref[0]
    mod = mod_ref[0]
    sh1, sc1 = mod[:, 0:d], mod[:, d:2 * d]
    ms = jnp.mean(x * x, axis=-1, keepdims=True)
    h = (x * lax.rsqrt(ms + EPS) * (g_ref[...] * (1.0 + sc1)) + sh1).astype(BF16)

    def proj(a, width):
        return _dot(h, w_ref[:, a:a + width])

    gq_ref[0] = proj(C_GQ, GLA_QK) * (GLA_DK ** -0.5)
    gk_ref[0] = proj(C_GK, GLA_QK)
    gv_ref[0] = proj(C_GV, GLA_V)
    gg_ref[0] = proj(C_GG, GLA_V)
    u = _dot(proj(C_GR, LANES).astype(BF16), wa2_ref[...]) + ba_ref[...]
    la_ref[0] = -_softplus(-u) * (1.0 / GLA_GATE_NORM)

    cos, sin = cos_ref[...], sin_ref[...]
    rq_ref[0] = _rope(proj(C_RQ, RET_W), cos, sin)
    rk_ref[0] = _rope(proj(C_RK, RET_W), cos, sin) * (RET_DK ** -0.5)
    rv_ref[0] = proj(C_RV, RET_W)
    rg_ref[0] = proj(C_RG, RET_W)

    sqT_ref[0] = (proj(C_SQ, SB_W) * (SB_DK ** -0.5)).T.astype(BF16)
    sk_ref[0] = proj(C_SK, SB_W).astype(BF16)
    svT_ref[0] = proj(C_SV, SB_W).T.astype(BF16)


def _input_projection(x, mod, g, w, wa2, ba, cos, sin):
    b, s, d = x.shape
    tm = min(TOKEN_TILE, s)
    tok = lambda width: pl.BlockSpec((1, tm, width), lambda bi, i: (bi, i, 0))
    tokT = lambda width: pl.BlockSpec((1, width, tm), lambda bi, i: (bi, 0, i))
    const = lambda shape: pl.BlockSpec(shape, lambda bi, i: tuple(0 for _ in shape))
    f32 = lambda width: jax.ShapeDtypeStruct((b, s, width), F32)
    return pl.pallas_call(
        _inproj_kernel,
        grid=(b, s // tm),
        in_specs=[tok(d),
                  pl.BlockSpec((1, 1, 6 * d), lambda bi, i: (bi, 0, 0)),
                  const((1, d)), const((d, IN_COLS_PADDED)), const((LANES, GLA_QK)), const((1, GLA_QK)),
                  pl.BlockSpec((tm, LANES), lambda bi, i: (i, 0)),
                  pl.BlockSpec((tm, LANES), lambda bi, i: (i, 0))],
        out_specs=[tok(GLA_QK), tok(GLA_QK), tok(GLA_V), tok(GLA_V), tok(GLA_QK),
                   tok(RET_W), tok(RET_W), tok(RET_W), tok(RET_W),
                   tokT(SB_W), tok(SB_W), tokT(SB_W)],
        out_shape=[f32(GLA_QK), f32(GLA_QK), f32(GLA_V), f32(GLA_V), f32(GLA_QK),
                   f32(RET_W), f32(RET_W), f32(RET_W), f32(RET_W),
                   jax.ShapeDtypeStruct((b, SB_W, s), BF16),
                   jax.ShapeDtypeStruct((b, s, SB_W), BF16),
                   jax.ShapeDtypeStruct((b, SB_W, s), BF16)],
        compiler_params=_params("arbitrary", "arbitrary"),
        name="input_projection",
    )(x, mod, g, w, wa2, ba, cos, sin)


def _gla_tables():
    c, nl = GLA_CHUNK, GLA_LEVELS
    t = np.arange(c)
    rows = []
    for l in range(nl):
        w = 1 << l
        start = (t // w) * w
        rows.append((t[None, :] >= start[:, None]) & (t[None, :] <= t[:, None]))
    for l in range(nl):
        w = 1 << l
        end = (t // w + 1) * w - 1
        rows.append((t[None, :] > t[:, None]) & (t[None, :] <= end[:, None]))
    rows.append(t[None, :] <= t[:, None])
    mcat = np.concatenate(rows, 0).astype(np.float32)
    mcat3 = np.concatenate([mcat] * 3, 1)
    masks = []
    for l in range(nl):
        w = 1 << l
        same = (t[:, None] // (2 * w)) == (t[None, :] // (2 * w))
        upper = ((t // w) % 2 == 1)[:, None]
        lower = ((t // w) % 2 == 0)[None, :]
        masks.append(np.tile(same & upper & lower, (1, GLA_HEADS)))
    masks.append(np.tile(np.eye(c, dtype=bool), (1, GLA_HEADS)))
    lmask = np.stack(masks).astype(np.float32)
    r = np.arange(GLA_HEADS * c)
    kmask = (r[:, None] // c == np.arange(GLA_QK)[None, :] // GLA_DK).astype(np.float32)
    vmask = (r[:, None] // c == np.arange(GLA_V)[None, :] // GLA_DV).astype(np.float32)
    smask = (np.arange(GLA_V)[:, None] // GLA_DV == np.arange(GLA_QK)[None, :] // GLA_DK).astype(np.float32)
    return (jnp.asarray(mcat3, BF16), jnp.asarray(lmask), jnp.asarray(kmask),
            jnp.asarray(vmask), jnp.asarray(smask))


def _split3(x):
    hi = x.astype(BF16)
    r1 = x - hi.astype(F32)
    mid = r1.astype(BF16)
    lo = (r1 - mid.astype(F32)).astype(BF16)
    return hi, mid, lo


def _gla_kernel(q_ref, k_ref, v_ref, la_ref, mcat_ref, lmask_ref, kmask_ref, vmask_ref, smask_ref,
                o_ref, st_ref, *, nchunk):
    c, nl = GLA_CHUNK, GLA_LEVELS

    @pl.when(pl.program_id(1) == 0)
    def _():
        st_ref[...] = jnp.zeros_like(st_ref)

    mcat = mcat_ref[...]
    kmask, vmask, smask = kmask_ref[...], vmask_ref[...], smask_ref[...]
    for ci in range(nchunk):
        sl = slice(ci * c, (ci + 1) * c)
        q, k, v, la = q_ref[0, sl, :], k_ref[0, sl, :], v_ref[0, sl, :], la_ref[0, sl, :]
        cs = _dot(mcat, jnp.concatenate(_split3(la), axis=0))
        e = jnp.exp(cs)
        scores = jnp.zeros((c, GLA_HEADS * c), F32)
        for l in range(nl + 1):
            if l < nl:
                qt = q * e[l * c:(l + 1) * c]
                kt = k * e[(nl + l) * c:(nl + l + 1) * c]
            else:
                qt, kt = q, k
            krows = (jnp.concatenate([kt] * GLA_HEADS, axis=0) * kmask).astype(BF16)
            scores = scores + _dot_nt(qt.astype(BF16), krows) * lmask_ref[l]
        vbd = (jnp.concatenate([v] * GLA_HEADS, axis=0) * vmask).astype(BF16)
        o = _dot(scores.astype(BF16), vbd)
        bcum = cs[2 * nl * c:(2 * nl + 1) * c]
        st = st_ref[...]
        o = o + _dot_nt((q * e[2 * nl * c:(2 * nl + 1) * c]).astype(BF16), st.astype(BF16))
        blast = bcum[c - 1:c, :]
        kd = k * jnp.exp(blast - bcum)
        st_ref[...] = st * jnp.exp(blast) + _dot_tn(v.astype(BF16), kd.astype(BF16)) * smask
        o_ref[0, sl, :] = o


def _gla(q, k, v, la, tables):
    b, s, _ = q.shape
    blk = min(GLA_BLOCK, s)
    tok = lambda width: pl.BlockSpec((1, blk, width), lambda bi, i: (bi, i, 0))
    const = lambda a: pl.BlockSpec(a.shape, lambda bi, i: tuple(0 for _ in a.shape))
    return pl.pallas_call(
        functools.partial(_gla_kernel, nchunk=blk // GLA_CHUNK),
        grid=(b, s // blk),
        in_specs=[tok(GLA_QK), tok(GLA_QK), tok(GLA_V), tok(GLA_QK)] + [const(a) for a in tables],
        out_specs=tok(GLA_V),
        out_shape=jax.ShapeDtypeStruct((b, s, GLA_V), F32),
        scratch_shapes=[pltpu.VMEM((GLA_V, GLA_QK), F32)],
        compiler_params=_params("arbitrary", "arbitrary"),
        name="gla_mixer",
    )(q, k, v, la, *tables)


def _ret_tables():
    c = RET_CHUNK
    gam = 1.0 - 2.0 ** (-5.0 - np.arange(RET_HEADS, dtype=np.float64))
    t = np.arange(c)
    rel = t[:, None] - t[None, :]
    dmat = np.where(rel[None] >= 0, gam[:, None, None] ** np.maximum(rel, 0)[None], 0.0)
    dall = np.concatenate(list(dmat), axis=1)
    xi = np.repeat((gam[None, :] ** (t[:, None] + 1.0)), RET_DK, axis=1)
    zeta = np.repeat((gam[None, :] ** (c - 1.0 - t[:, None])), RET_DK, axis=1)
    gc = np.repeat(gam ** c, RET_DK)[:, None] * np.ones((1, RET_W))
    r = np.arange(RET_HEADS * c)
    hmask = (r[:, None] // c == np.arange(RET_W)[None, :] // RET_DK)
    smask = (np.arange(RET_W)[:, None] // RET_DK == np.arange(RET_W)[None, :] // RET_DV)
    return tuple(jnp.asarray(a, F32) for a in (dall, xi, zeta, gc, hmask, smask))


def _ret_kernel(q_ref, k_ref, v_ref, dall_ref, xi_ref, zeta_ref, gc_ref, hmask_ref, smask_ref,
                o_ref, r_ref):
    @pl.when(pl.program_id(1) == 0)
    def _():
        r_ref[...] = jnp.zeros_like(r_ref)

    q, k, v = q_ref[0], k_ref[0], v_ref[0]
    hmask = hmask_ref[...]
    krows = (jnp.concatenate([k] * RET_HEADS, axis=0) * hmask).astype(BF16)
    scores = _dot_nt(q.astype(BF16), krows) * dall_ref[...]
    vbd = (jnp.concatenate([v] * RET_HEADS, axis=0) * hmask).astype(BF16)
    o = _dot(scores.astype(BF16), vbd)
    r = r_ref[...]
    o = o + _dot((q * xi_ref[...]).astype(BF16), r.astype(BF16))
    upd = _dot_tn((k * zeta_ref[...]).astype(BF16), v.astype(BF16))
    r_ref[...] = r * gc_ref[...] + upd * smask_ref[...]
    o_ref[0] = o


def _retention(q, k, v, tables):
    b, s, _ = q.shape
    c = RET_CHUNK
    tok = pl.BlockSpec((1, c, RET_W), lambda bi, i: (bi, i, 0))
    const = lambda a: pl.BlockSpec(a.shape, lambda bi, i: tuple(0 for _ in a.shape))
    return pl.pallas_call(
        _ret_kernel,
        grid=(b, s // c),
        in_specs=[tok, tok, tok] + [const(a) for a in tables],
        out_specs=tok,
        out_shape=jax.ShapeDtypeStruct((b, s, RET_W), F32),
        scratch_shapes=[pltpu.VMEM((RET_W, RET_W), F32)],
        compiler_params=_params("arbitrary", "arbitrary"),
        name="retention_mixer",
    )(q, k, v, *tables)


def _sb_kernel(qT_ref, k_ref, vT_ref, tri_ref, o_ref):
    t = SB_TILE
    i = pl.program_id(2)
    qT = qT_ref[0]
    row = lax.broadcasted_iota(jnp.int32, qT.shape, 0)
    zero = jnp.zeros_like(qT)
    q_heads = (jnp.where(row < SB_DK, qT, zero), jnp.where(row >= SB_DK, qT, zero))
    tri = tri_ref[...]
    s_idx = lax.broadcasted_iota(jnp.int32, (t, t), 0)
    t_idx = lax.broadcasted_iota(jnp.int32, (t, t), 1)
    causal = s_idx < t_idx

    def tile(j, carry, masked):
        start = pl.multiple_of(j * t, t)
        k2 = k_ref[0, pl.ds(start, t), :]
        vT2 = vT_ref[0, :, pl.ds(start, t)]
        out = []
        for hd in range(2):
            cr, acc = carry[hd]
            z = _dot(k2, q_heads[hd])
            sp = _softplus(z)
            if masked:
                sp = jnp.where(causal, sp, 0.0)
            hi = sp.astype(BF16)
            lo = (sp - hi.astype(F32)).astype(BF16)
            r = _dot(tri, hi) + _dot(tri, lo)
            p = jnp.exp(z - r - cr)
            if masked:
                p = jnp.where(causal, p, 0.0)
            acc = acc + _dot(vT2[hd * SB_DV:(hd + 1) * SB_DV, :], p.astype(BF16))
            out.append((cr + r[0:1, :], acc))
        return tuple(out)

    init = tuple((jnp.zeros((1, t), F32), jnp.zeros((SB_DV, t), F32)) for _ in range(2))
    carry = tile(i, init, True)
    carry = lax.fori_loop(0, i, lambda n, cy: tile(i - 1 - n, cy, False), carry)
    o_ref[0] = jnp.concatenate([carry[0][1].T, carry[1][1].T], axis=1).astype(BF16)


def _stick_breaking(qT, k, vT):
    b, w, s = qT.shape
    t = SB_TILE
    pairs = w // LANES
    r = np.arange(t)
    tri = jnp.asarray(r[None, :] >= r[:, None], BF16)
    return pl.pallas_call(
        _sb_kernel,
        grid=(b, pairs, s // t),
        in_specs=[pl.BlockSpec((1, LANES, t), lambda bi, p, i: (bi, p, i)),
                  pl.BlockSpec((1, s, LANES), lambda bi, p, i: (bi, 0, p)),
                  pl.BlockSpec((1, LANES, s), lambda bi, p, i: (bi, p, 0)),
                  pl.BlockSpec((t, t), lambda bi, p, i: (0, 0))],
        out_specs=pl.BlockSpec((1, t, LANES), lambda bi, p, i: (bi, i, p)),
        out_shape=jax.ShapeDtypeStruct((b, s, w), BF16),
        compiler_params=_params("arbitrary", "arbitrary", "arbitrary"),
        name="stick_breaking_mixer",
    )(qT, k, vT, tri)


def _head_norm_gate(o, g, gate, bd):
    sq = o * o
    hi = sq.astype(BF16)
    lo = (sq - hi.astype(F32)).astype(BF16)
    ms = _dot(hi, bd) + _dot(lo, bd)
    return o * lax.rsqrt(ms + EPS) * g * (gate * _sigmoid(gate))


def _ffn_kernel(x_ref, mod_ref, og_ref, gg_ref, or_ref, rg_ref, osb_ref, gng_ref, rng_ref, bd_ref,
                wout_ref, n2g_ref, wg_ref, wu_ref, wd_ref, fg_ref, o_ref, p_ref, *, final):
    d = D_MODEL
    x = x_ref[0]
    mod = mod_ref[0]
    g1, sh2, sc2, g2 = mod[:, 2 * d:3 * d], mod[:, 3 * d:4 * d], mod[:, 4 * d:5 * d], mod[:, 5 * d:6 * d]
    bd = bd_ref[...]
    o_gla = _head_norm_gate(og_ref[0], gng_ref[...], gg_ref[0], bd)
    o_ret = _head_norm_gate(or_ref[0], rng_ref[...], rg_ref[0], bd)
    o = jnp.concatenate([o_gla.astype(BF16), o_ret.astype(BF16), osb_ref[0]], axis=1)
    x1 = x + g1 * _dot(o, wout_ref[...])
    ms = jnp.mean(x1 * x1, axis=-1, keepdims=True)
    h = (x1 * lax.rsqrt(ms + EPS) * (n2g_ref[...] * (1.0 + sc2)) + sh2).astype(BF16)
    for c0 in range(0, D_FF, FF_CHUNK):
        a = _dot(h, wg_ref[:, c0:c0 + FF_CHUNK])
        u = _dot(h, wu_ref[:, c0:c0 + FF_CHUNK])
        p_ref[:, c0:c0 + FF_CHUNK] = (a * _sigmoid(a) * u).astype(BF16)
    x2 = x1 + g2 * _dot(p_ref[...], wd_ref[...])
    if final:
        ms = jnp.mean(x2 * x2, axis=-1, keepdims=True)
        x2 = x2 * lax.rsqrt(ms + EPS) * fg_ref[...]
    o_ref[0] = x2


def _out_ffn(x, mod, og, gg, orr, rg, osb, gng, rng, bd, wout, n2g, wg, wu, wd, fg, final):
    b, s, d = x.shape
    tm = min(TOKEN_TILE, s)
    tok = lambda width: pl.BlockSpec((1, tm, width), lambda bi, i: (bi, i, 0))
    const = lambda a: pl.BlockSpec(a.shape, lambda bi, i: tuple(0 for _ in a.shape),
                                   pipeline_mode=pl.Buffered(1))
    return pl.pallas_call(
        functools.partial(_ffn_kernel, final=final),
        grid=(b, s // tm),
        in_specs=[tok(d), pl.BlockSpec((1, 1, 6 * d), lambda bi, i: (bi, 0, 0)),
                  tok(GLA_V), tok(GLA_V), tok(RET_W), tok(RET_W), tok(SB_W),
                  const(gng), const(rng), const(bd), const(wout), const(n2g),
                  const(wg), const(wu), const(wd), const(fg)],
        out_specs=tok(d),
        out_shape=jax.ShapeDtypeStruct((b, s, d), F32),
        scratch_shapes=[pltpu.VMEM((tm, D_FF), BF16)],
        compiler_params=_params("arbitrary", "arbitrary"),
        name="out_projection_ffn",
    )(x, mod, og, gg, orr, rg, osb, gng, rng, bd, wout, n2g, wg, wu, wd, fg)


def _rope_tables(s):
    inv = ROPE_BASE ** (-np.arange(0, RET_DK, 2, dtype=np.float64) / RET_DK)
    ang = np.arange(s, dtype=np.float64)[:, None] * inv[None, :]
    cos = np.repeat(np.cos(ang), 2, axis=1)
    sin = np.repeat(np.sin(ang), 2, axis=1) * np.tile([-1.0, 1.0], RET_DK // 2)[None, :]
    reps = LANES // RET_DK
    return jnp.asarray(np.tile(cos, (1, reps)), F32), jnp.asarray(np.tile(sin, (1, reps)), F32)


def kernel(x, c, ada_w, ada_b, norm1_g, norm2_g, w_in, gla_wa2, gla_ba, gla_norm_g, ret_norm_g,
           w_out, ffn_wg, ffn_wu, ffn_wd, final_g):
    b, s, d = x.shape
    mods = _modulation(c, ada_w, ada_b)
    cos, sin = _rope_tables(s)
    gla_tables = _gla_tables()
    ret_tables = _ret_tables()
    hd = np.arange(GLA_V)
    bd = jnp.asarray((hd[:, None] // GLA_DV == hd[None, :] // GLA_DV) / float(GLA_DV), BF16)
    gr0 = C_RQ
    for l in range(DEPTH):
        w = w_in[l]
        w_perm = jnp.concatenate(
            [w[:, :gr0], w[:, gr0 + GLA_RANK:], w[:, gr0:gr0 + GLA_RANK],
             jnp.zeros((d, IN_COLS_PADDED - C_GR - GLA_RANK), w.dtype)], axis=1).astype(BF16)
        wa2 = jnp.concatenate([gla_wa2[l], jnp.zeros((LANES - GLA_RANK, GLA_QK), F32)], axis=0).astype(BF16)
        mod = mods[l].reshape(b, 1, 6 * d)
        (gq, gk, gv, gg, la, rq, rk, rv, rg, sqT, sk, svT) = _input_projection(
            x, mod, norm1_g[l].reshape(1, d), w_perm, wa2, gla_ba[l].reshape(1, GLA_QK), cos, sin)
        o_gla = _gla(gq, gk, gv, la, gla_tables)
        o_ret = _retention(rq, rk, rv, ret_tables)
        o_sb = _stick_breaking(sqT, sk, svT)
        x = _out_ffn(x, mod, o_gla, gg, o_ret, rg, o_sb,
                     jnp.tile(gla_norm_g[l], GLA_HEADS).reshape(1, GLA_V),
                     jnp.tile(ret_norm_g[l], RET_HEADS).reshape(1, RET_W), bd,
                     w_out[l].astype(BF16), norm2_g[l].reshape(1, d),
                     ffn_wg[l].astype(BF16), ffn_wu[l].astype(BF16), ffn_wd[l].astype(BF16),
                     final_g.reshape(1, d), l == DEPTH - 1)
    return x
```

```python
import functools

import jax
import jax.numpy as jnp
import numpy as np
from jax import lax
from jax.experimental import pallas as pl
from jax.experimental.pallas import tpu as pltpu

F32 = jnp.float32
BF16 = jnp.bfloat16

D_MODEL = 1024
DEPTH = 2
GLA_HEADS, GLA_DK, GLA_DV, GLA_RANK = 4, 32, 64, 16
GLA_GATE_NORM = 16.0
RET_HEADS, RET_DK, RET_DV = 4, 64, 64
ROPE_BASE = 10000.0
SB_HEADS, SB_DK, SB_DV = 8, 64, 64
D_FF = 2816
EPS = 1e-6

GLA_QK = GLA_HEADS * GLA_DK
GLA_V = GLA_HEADS * GLA_DV
RET_W = RET_HEADS * RET_DK
SB_W = SB_HEADS * SB_DK
C_GQ, C_GK, C_GV, C_GG = 0, 128, 256, 512
C_RQ, C_RK, C_RV, C_RG = 768, 1024, 1280, 1536
C_SQ, C_SK, C_SV = 1792, 2304, 2816
C_GR = 3328
IN_COLS_PADDED = 3456

LANES = 128
GLA_CHUNK = 64
GLA_LEVELS = 6
GLA_BLOCK = 256
RET_CHUNK = 256
SB_TQ = 512
SB_TK = 256
LOG2E = 1.4426950408889634
TOKEN_TILE = 512
FF_CHUNK = 256
VMEM_LIMIT = 56 * 1024 * 1024


def _dot(a, b):
    return jnp.dot(a, b, preferred_element_type=F32)


def _dot_nt(a, b):
    return lax.dot_general(a, b, (((1,), (1,)), ((), ())), preferred_element_type=F32)


def _dot_tn(a, b):
    return lax.dot_general(a, b, (((0,), (0,)), ((), ())), preferred_element_type=F32)


def _sigmoid(x):
    return 1.0 / (1.0 + jnp.exp(-x))


def _softplus(x):
    return jnp.maximum(x, 0.0) + jnp.log(1.0 + jnp.exp(-jnp.abs(x)))


def _params(*sem):
    return pltpu.CompilerParams(dimension_semantics=sem, vmem_limit_bytes=VMEM_LIMIT)


def _mod_kernel(c_ref, w_ref, b_ref, o_ref):
    c = c_ref[...]
    cs = c * _sigmoid(c)
    o_ref[0] = jnp.dot(cs, w_ref[0], preferred_element_type=F32,
                       precision=lax.Precision.HIGHEST) + b_ref[0]


def _modulation(c, ada_w, ada_b):
    depth, d, n = ada_w.shape
    b = c.shape[0]
    tn = 1536
    return pl.pallas_call(
        _mod_kernel,
        grid=(depth, n // tn),
        in_specs=[pl.BlockSpec((b, d), lambda l, j: (0, 0)),
                  pl.BlockSpec((1, d, tn), lambda l, j: (l, 0, j)),
                  pl.BlockSpec((1, 1, tn), lambda l, j: (l, 0, j))],
        out_specs=pl.BlockSpec((1, b, tn), lambda l, j: (l, 0, j)),
        out_shape=jax.ShapeDtypeStruct((depth, b, n), F32),
        compiler_params=_params("arbitrary", "arbitrary"),
        name="adaln_modulation",
    )(c, ada_w, ada_b.reshape(depth, 1, n))


def _rope(t, cos, sin):
    lane = lax.broadcasted_iota(jnp.int32, cos.shape, 1)
    even = (lane & 1) == 0
    outs = []
    for c in range(t.shape[1] // LANES):
        tc = t[:, c * LANES:(c + 1) * LANES]
        nxt = pltpu.roll(tc, LANES - 1, 1)
        prv = pltpu.roll(tc, 1, 1)
        outs.append(tc * cos + jnp.where(even, nxt, prv) * sin)
    return jnp.concatenate(outs, axis=1)


def _inproj_kernel(x_ref, mod_ref, g_ref, w_ref, wa2_ref, ba_ref, cos_ref, sin_ref,
                   gq_ref, gk_ref, gv_ref, gg_ref, la_ref,
                   rq_ref, rk_ref, rv_ref, rg_ref, sq_ref, skT_ref, sv_ref):
    d = D_MODEL
    x = x_ref[0]
    mod = mod_ref[0]
    sh1, sc1 = mod[:, 0:d], mod[:, d:2 * d]
    ms = jnp.mean(x * x, axis=-1, keepdims=True)
    h = (x * lax.rsqrt(ms + EPS) * (g_ref[...] * (1.0 + sc1)) + sh1).astype(BF16)

    def proj(a, width):
        return _dot(h, w_ref[:, a:a + width])

    gq_ref[0] = proj(C_GQ, GLA_QK) * (GLA_DK ** -0.5)
    gk_ref[0] = proj(C_GK, GLA_QK)
    gv_ref[0] = proj(C_GV, GLA_V)
    gg_ref[0] = proj(C_GG, GLA_V)
    u = _dot(proj(C_GR, LANES).astype(BF16), wa2_ref[...]) + ba_ref[...]
    la_ref[0] = -_softplus(-u) * (1.0 / GLA_GATE_NORM)

    cos, sin = cos_ref[...], sin_ref[...]
    rq_ref[0] = _rope(proj(C_RQ, RET_W), cos, sin)
    rk_ref[0] = _rope(proj(C_RK, RET_W), cos, sin) * (RET_DK ** -0.5)
    rv_ref[0] = proj(C_RV, RET_W)
    rg_ref[0] = proj(C_RG, RET_W)

    sq_ref[0] = (proj(C_SQ, SB_W) * (LOG2E * SB_DK ** -0.5)).astype(BF16)
    skT_ref[0] = proj(C_SK, SB_W).T.astype(BF16)
    sv_ref[0] = proj(C_SV, SB_W).astype(BF16)


def _input_projection(x, mod, g, w, wa2, ba, cos, sin):
    b, s, d = x.shape
    tm = min(TOKEN_TILE, s)
    tok = lambda width: pl.BlockSpec((1, tm, width), lambda bi, i: (bi, i, 0))
    tokT = lambda width: pl.BlockSpec((1, width, tm), lambda bi, i: (bi, 0, i))
    const = lambda shape: pl.BlockSpec(shape, lambda bi, i: tuple(0 for _ in shape))
    f32 = lambda width: jax.ShapeDtypeStruct((b, s, width), F32)
    return pl.pallas_call(
        _inproj_kernel,
        grid=(b, s // tm),
        in_specs=[tok(d),
                  pl.BlockSpec((1, 1, 6 * d), lambda bi, i: (bi, 0, 0)),
                  const((1, d)), const((d, IN_COLS_PADDED)), const((LANES, GLA_QK)), const((1, GLA_QK)),
                  pl.BlockSpec((tm, LANES), lambda bi, i: (i, 0)),
                  pl.BlockSpec((tm, LANES), lambda bi, i: (i, 0))],
        out_specs=[tok(GLA_QK), tok(GLA_QK), tok(GLA_V), tok(GLA_V), tok(GLA_QK),
                   tok(RET_W), tok(RET_W), tok(RET_W), tok(RET_W),
                   tok(SB_W), tokT(SB_W), tok(SB_W)],
        out_shape=[f32(GLA_QK), f32(GLA_QK), f32(GLA_V), f32(GLA_V), f32(GLA_QK),
                   f32(RET_W), f32(RET_W), f32(RET_W), f32(RET_W),
                   jax.ShapeDtypeStruct((b, s, SB_W), BF16),
                   jax.ShapeDtypeStruct((b, SB_W, s), BF16),
                   jax.ShapeDtypeStruct((b, s, SB_W), BF16)],
        compiler_params=_params("arbitrary", "arbitrary"),
        name="input_projection",
    )(x, mod, g, w, wa2, ba, cos, sin)


def _gla_tables():
    c, nl = GLA_CHUNK, GLA_LEVELS
    t = np.arange(c)
    rows = []
    for l in range(nl):
        w = 1 << l
        start = (t // w) * w
        rows.append((t[None, :] >= start[:, None]) & (t[None, :] <= t[:, None]))
    for l in range(nl):
        w = 1 << l
        end = (t // w + 1) * w - 1
        rows.append((t[None, :] > t[:, None]) & (t[None, :] <= end[:, None]))
    rows.append(t[None, :] <= t[:, None])
    mcat = np.concatenate(rows, 0).astype(np.float32)
    mcat3 = np.concatenate([mcat] * 3, 1)
    masks = []
    for l in range(nl):
        w = 1 << l
        same = (t[:, None] // (2 * w)) == (t[None, :] // (2 * w))
        upper = ((t // w) % 2 == 1)[:, None]
        lower = ((t // w) % 2 == 0)[None, :]
        masks.append(np.tile(same & upper & lower, (1, GLA_HEADS)))
    masks.append(np.tile(np.eye(c, dtype=bool), (1, GLA_HEADS)))
    lmask = np.stack(masks).astype(np.float32)
    r = np.arange(GLA_HEADS * c)
    kmask = (r[:, None] // c == np.arange(GLA_QK)[None, :] // GLA_DK).astype(np.float32)
    vmask = (r[:, None] // c == np.arange(GLA_V)[None, :] // GLA_DV).astype(np.float32)
    smask = (np.arange(GLA_V)[:, None] // GLA_DV == np.arange(GLA_QK)[None, :] // GLA_DK).astype(np.float32)
    return (jnp.asarray(mcat3, BF16), jnp.asarray(lmask), jnp.asarray(kmask),
            jnp.asarray(vmask), jnp.asarray(smask))


def _split3(x):
    hi = x.astype(BF16)
    r1 = x - hi.astype(F32)
    mid = r1.astype(BF16)
    lo = (r1 - mid.astype(F32)).astype(BF16)
    return hi, mid, lo


def _gla_kernel(q_ref, k_ref, v_ref, la_ref, mcat_ref, lmask_ref, kmask_ref, vmask_ref, smask_ref,
                o_ref, st_ref, *, nchunk):
    c, nl = GLA_CHUNK, GLA_LEVELS

    @pl.when(pl.program_id(1) == 0)
    def _():
        st_ref[...] = jnp.zeros_like(st_ref)

    mcat = mcat_ref[...]
    kmask, vmask, smask = kmask_ref[...], vmask_ref[...], smask_ref[...]
    for ci in range(nchunk):
        sl = slice(ci * c, (ci + 1) * c)
        q, k, v, la = q_ref[0, sl, :], k_ref[0, sl, :], v_ref[0, sl, :], la_ref[0, sl, :]
        cs = _dot(mcat, jnp.concatenate(_split3(la), axis=0))
        e = jnp.exp(cs)
        scores = jnp.zeros((c, GLA_HEADS * c), F32)
        for l in range(nl + 1):
            if l < nl:
                qt = q * e[l * c:(l + 1) * c]
                kt = k * e[(nl + l) * c:(nl + l + 1) * c]
            else:
                qt, kt = q, k
            krows = (jnp.concatenate([kt] * GLA_HEADS, axis=0) * kmask).astype(BF16)
            scores = scores + _dot_nt(qt.astype(BF16), krows) * lmask_ref[l]
        vbd = (jnp.concatenate([v] * GLA_HEADS, axis=0) * vmask).astype(BF16)
        o = _dot(scores.astype(BF16), vbd)
        bcum = cs[2 * nl * c:(2 * nl + 1) * c]
        st = st_ref[...]
        o = o + _dot_nt((q * e[2 * nl * c:(2 * nl + 1) * c]).astype(BF16), st.astype(BF16))
        blast = bcum[c - 1:c, :]
        kd = k * jnp.exp(blast - bcum)
        st_ref[...] = st * jnp.exp(blast) + _dot_tn(v.astype(BF16), kd.astype(BF16)) * smask
        o_ref[0, sl, :] = o


def _gla(q, k, v, la, tables):
    b, s, _ = q.shape
    blk = min(GLA_BLOCK, s)
    tok = lambda width: pl.BlockSpec((1, blk, width), lambda bi, i: (bi, i, 0))
    const = lambda a: pl.BlockSpec(a.shape, lambda bi, i: tuple(0 for _ in a.shape))
    return pl.pallas_call(
        functools.partial(_gla_kernel, nchunk=blk // GLA_CHUNK),
        grid=(b, s // blk),
        in_specs=[tok(GLA_QK), tok(GLA_QK), tok(GLA_V), tok(GLA_QK)] + [const(a) for a in tables],
        out_specs=tok(GLA_V),
        out_shape=jax.ShapeDtypeStruct((b, s, GLA_V), F32),
        scratch_shapes=[pltpu.VMEM((GLA_V, GLA_QK), F32)],
        compiler_params=_params("arbitrary", "arbitrary"),
        name="gla_mixer",
    )(q, k, v, la, *tables)


def _ret_tables():
    c = RET_CHUNK
    gam = 1.0 - 2.0 ** (-5.0 - np.arange(RET_HEADS, dtype=np.float64))
    t = np.arange(c)
    rel = t[:, None] - t[None, :]
    dmat = np.where(rel[None] >= 0, gam[:, None, None] ** np.maximum(rel, 0)[None], 0.0)
    dall = np.concatenate(list(dmat), axis=1)
    xi = np.repeat((gam[None, :] ** (t[:, None] + 1.0)), RET_DK, axis=1)
    zeta = np.repeat((gam[None, :] ** (c - 1.0 - t[:, None])), RET_DK, axis=1)
    gc = np.repeat(gam ** c, RET_DK)[:, None] * np.ones((1, RET_W))
    r = np.arange(RET_HEADS * c)
    hmask = (r[:, None] // c == np.arange(RET_W)[None, :] // RET_DK)
    smask = (np.arange(RET_W)[:, None] // RET_DK == np.arange(RET_W)[None, :] // RET_DV)
    return tuple(jnp.asarray(a, F32) for a in (dall, xi, zeta, gc, hmask, smask))


def _ret_kernel(q_ref, k_ref, v_ref, dall_ref, xi_ref, zeta_ref, gc_ref, hmask_ref, smask_ref,
                o_ref, r_ref):
    @pl.when(pl.program_id(1) == 0)
    def _():
        r_ref[...] = jnp.zeros_like(r_ref)

    q, k, v = q_ref[0], k_ref[0], v_ref[0]
    hmask = hmask_ref[...]
    krows = (jnp.concatenate([k] * RET_HEADS, axis=0) * hmask).astype(BF16)
    scores = _dot_nt(q.astype(BF16), krows) * dall_ref[...]
    vbd = (jnp.concatenate([v] * RET_HEADS, axis=0) * hmask).astype(BF16)
    o = _dot(scores.astype(BF16), vbd)
    r = r_ref[...]
    o = o + _dot((q * xi_ref[...]).astype(BF16), r.astype(BF16))
    upd = _dot_tn((k * zeta_ref[...]).astype(BF16), v.astype(BF16))
    r_ref[...] = r * gc_ref[...] + upd * smask_ref[...]
    o_ref[0] = o


def _retention(q, k, v, tables):
    b, s, _ = q.shape
    c = RET_CHUNK
    tok = pl.BlockSpec((1, c, RET_W), lambda bi, i: (bi, i, 0))
    const = lambda a: pl.BlockSpec(a.shape, lambda bi, i: tuple(0 for _ in a.shape))
    return pl.pallas_call(
        _ret_kernel,
        grid=(b, s // c),
        in_specs=[tok, tok, tok] + [const(a) for a in tables],
        out_specs=tok,
        out_shape=jax.ShapeDtypeStruct((b, s, RET_W), F32),
        scratch_shapes=[pltpu.VMEM((RET_W, RET_W), F32)],
        compiler_params=_params("arbitrary", "arbitrary"),
        name="retention_mixer",
    )(q, k, v, *tables)


def _sb_kernel(q_ref, kT_ref, v_ref, w_ref, o_ref, acc_ref, c_ref):
    tq, tk, half = SB_TQ, SB_TK, LANES
    i = pl.program_id(2)
    q2 = q_ref[0]
    lane = lax.broadcasted_iota(jnp.int32, q2.shape, 1)
    zero = jnp.zeros_like(q2)
    qs = (jnp.where(lane < SB_DK, q2, zero), jnp.where(lane >= SB_DK, q2, zero))
    acc_ref[...] = jnp.zeros_like(acc_ref)
    c_ref[...] = jnp.zeros_like(c_ref)
    w = w_ref[...]
    row_t = lax.broadcasted_iota(jnp.int32, (tq, half), 0) + i * tq
    lane_s = lax.broadcasted_iota(jnp.int32, (tq, half), 1)

    def tile(j, masked):
        start = pl.multiple_of(j * tk, tk)
        kT2 = kT_ref[0, :, pl.ds(start, tk)]
        v2 = v_ref[0, pl.ds(start, tk), :]
        for hd in range(2):
            z = _dot(qs[hd], kT2)
            c = c_ref[hd]
            ps = [None] * (tk // half)
            for h in reversed(range(tk // half)):
                zh = z[:, h * half:(h + 1) * half]
                nabs = pltpu.bitcast(pltpu.bitcast(zh, jnp.uint32) | jnp.uint32(0x80000000), F32)
                sp = jnp.log(1.0 + jnp.exp2(nabs)) * LOG2E + jnp.maximum(zh, 0.0)
                if masked:
                    causal = (lane_s + (start + h * half)) < row_t
                    sp = jnp.where(causal, sp, 0.0)
                hi = sp.astype(BF16)
                lo = (sp - hi.astype(F32)).astype(BF16)
                out = _dot(jnp.concatenate([hi, lo], axis=1), w)
                p = jnp.exp2(zh - (out[:, :half] + c))
                c = c + out[:, half:]
                if masked:
                    p = jnp.where(causal, p, 0.0)
                ps[h] = p.astype(BF16)
            c_ref[hd] = c
            acc_ref[hd] += _dot(jnp.concatenate(ps, axis=1), v2)

    nk = tq // tk
    for d in range(nk):
        tile(i * nk + (nk - 1 - d), True)

    def body(n, carry):
        tile(i * nk - 1 - 2 * n, False)
        tile(i * nk - 2 - 2 * n, False)
        return carry

    lax.fori_loop(0, i * (nk // 2), body, 0)
    o_ref[0] = jnp.where(lane < SB_DK, acc_ref[0], acc_ref[1]).astype(BF16)


def _stick_breaking(q, kT, v):
    b, s, w = q.shape
    tq = SB_TQ
    pairs = w // LANES
    r = np.arange(LANES)
    tri_ones = np.concatenate([r[:, None] >= r[None, :], np.ones((LANES, LANES), bool)], axis=1)
    wmat = jnp.asarray(np.concatenate([tri_ones, tri_ones], axis=0), BF16)
    return pl.pallas_call(
        _sb_kernel,
        grid=(b, pairs, s // tq),
        in_specs=[pl.BlockSpec((1, tq, LANES), lambda bi, p, i: (bi, i, p)),
                  pl.BlockSpec((1, LANES, s), lambda bi, p, i: (bi, p, 0)),
                  pl.BlockSpec((1, s, LANES), lambda bi, p, i: (bi, 0, p)),
                  pl.BlockSpec(wmat.shape, lambda bi, p, i: (0, 0))],
        out_specs=pl.BlockSpec((1, tq, LANES), lambda bi, p, i: (bi, i, p)),
        out_shape=jax.ShapeDtypeStruct((b, s, w), BF16),
        scratch_shapes=[pltpu.VMEM((2, tq, LANES), F32), pltpu.VMEM((2, tq, LANES), F32)],
        compiler_params=_params("arbitrary", "arbitrary", "arbitrary"),
        name="stick_breaking_mixer",
    )(q, kT, v, wmat)


def _head_norm_gate(o, g, gate, bd):
    sq = o * o
    hi = sq.astype(BF16)
    lo = (sq - hi.astype(F32)).astype(BF16)
    ms = _dot(hi, bd) + _dot(lo, bd)
    return o * lax.rsqrt(ms + EPS) * g * (gate * _sigmoid(gate))


def _ffn_kernel(x_ref, mod_ref, og_ref, gg_ref, or_ref, rg_ref, osb_ref, gng_ref, rng_ref, bd_ref,
                wout_ref, n2g_ref, wg_ref, wu_ref, wd_ref, fg_ref, o_ref, p_ref, *, final):
    d = D_MODEL
    x = x_ref[0]
    mod = mod_ref[0]
    g1, sh2, sc2, g2 = mod[:, 2 * d:3 * d], mod[:, 3 * d:4 * d], mod[:, 4 * d:5 * d], mod[:, 5 * d:6 * d]
    bd = bd_ref[...]
    o_gla = _head_norm_gate(og_ref[0], gng_ref[...], gg_ref[0], bd)
    o_ret = _head_norm_gate(or_ref[0], rng_ref[...], rg_ref[0], bd)
    o = jnp.concatenate([o_gla.astype(BF16), o_ret.astype(BF16), osb_ref[0]], axis=1)
    x1 = x + g1 * _dot(o, wout_ref[...])
    ms = jnp.mean(x1 * x1, axis=-1, keepdims=True)
    h = (x1 * lax.rsqrt(ms + EPS) * (n2g_ref[...] * (1.0 + sc2)) + sh2).astype(BF16)
    for c0 in range(0, D_FF, FF_CHUNK):
        a = _dot(h, wg_ref[:, c0:c0 + FF_CHUNK])
        u = _dot(h, wu_ref[:, c0:c0 + FF_CHUNK])
        p_ref[:, c0:c0 + FF_CHUNK] = (a * _sigmoid(a) * u).astype(BF16)
    x2 = x1 + g2 * _dot(p_ref[...], wd_ref[...])
    if final:
        ms = jnp.mean(x2 * x2, axis=-1, keepdims=True)
        x2 = x2 * lax.rsqrt(ms + EPS) * fg_ref[...]
    o_ref[0] = x2


def _out_ffn(x, mod, og, gg, orr, rg, osb, gng, rng, bd, wout, n2g, wg, wu, wd, fg, final):
    b, s, d = x.shape
    tm = min(TOKEN_TILE, s)
    tok = lambda width: pl.BlockSpec((1, tm, width), lambda bi, i: (bi, i, 0))
    const = lambda a: pl.BlockSpec(a.shape, lambda bi, i: tuple(0 for _ in a.shape),
                                   pipeline_mode=pl.Buffered(1))
    return pl.pallas_call(
        functools.partial(_ffn_kernel, final=final),
        grid=(b, s // tm),
        in_specs=[tok(d), pl.BlockSpec((1, 1, 6 * d), lambda bi, i: (bi, 0, 0)),
                  tok(GLA_V), tok(GLA_V), tok(RET_W), tok(RET_W), tok(SB_W),
                  const(gng), const(rng), const(bd), const(wout), const(n2g),
                  const(wg), const(wu), const(wd), const(fg)],
        out_specs=tok(d),
        out_shape=jax.ShapeDtypeStruct((b, s, d), F32),
        scratch_shapes=[pltpu.VMEM((tm, D_FF), BF16)],
        compiler_params=_params("arbitrary", "arbitrary"),
        name="out_projection_ffn",
    )(x, mod, og, gg, orr, rg, osb, gng, rng, bd, wout, n2g, wg, wu, wd, fg)


def _rope_tables(s):
    inv = ROPE_BASE ** (-np.arange(0, RET_DK, 2, dtype=np.float64) / RET_DK)
    ang = np.arange(s, dtype=np.float64)[:, None] * inv[None, :]
    cos = np.repeat(np.cos(ang), 2, axis=1)
    sin = np.repeat(np.sin(ang), 2, axis=1) * np.tile([-1.0, 1.0], RET_DK // 2)[None, :]
    reps = LANES // RET_DK
    return jnp.asarray(np.tile(cos, (1, reps)), F32), jnp.asarray(np.tile(sin, (1, reps)), F32)


def kernel(x, c, ada_w, ada_b, norm1_g, norm2_g, w_in, gla_wa2, gla_ba, gla_norm_g, ret_norm_g,
           w_out, ffn_wg, ffn_wu, ffn_wd, final_g):
    b, s, d = x.shape
    mods = _modulation(c, ada_w, ada_b)
    cos, sin = _rope_tables(s)
    gla_tables = _gla_tables()
    ret_tables = _ret_tables()
    hd = np.arange(GLA_V)
    bd = jnp.asarray((hd[:, None] // GLA_DV == hd[None, :] // GLA_DV) / float(GLA_DV), BF16)
    gr0 = C_RQ
    for l in range(DEPTH):
        w = w_in[l]
        w_perm = jnp.concatenate(
            [w[:, :gr0], w[:, gr0 + GLA_RANK:], w[:, gr0:gr0 + GLA_RANK],
             jnp.zeros((d, IN_COLS_PADDED - C_GR - GLA_RANK), w.dtype)], axis=1).astype(BF16)
        wa2 = jnp.concatenate([gla_wa2[l], jnp.zeros((LANES - GLA_RANK, GLA_QK), F32)], axis=0).astype(BF16)
        mod = mods[l].reshape(b, 1, 6 * d)
        (gq, gk, gv, gg, la, rq, rk, rv, rg, sq, skT, sv) = _input_projection(
            x, mod, norm1_g[l].reshape(1, d), w_perm, wa2, gla_ba[l].reshape(1, GLA_QK), cos, sin)
        o_gla = _gla(gq, gk, gv, la, gla_tables)
        o_ret = _retention(rq, rk, rv, ret_tables)
        o_sb = _stick_breaking(sq, skT, sv)
        x = _out_ffn(x, mod, o_gla, gg, o_ret, rg, o_sb,
                     jnp.tile(gla_norm_g[l], GLA_HEADS).reshape(1, GLA_V),
                     jnp.tile(ret_norm_g[l], RET_HEADS).reshape(1, RET_W), bd,
                     w_out[l].astype(BF16), norm2_g[l].reshape(1, d),
                     ffn_wg[l].astype(BF16), ffn_wu[l].astype(BF16), ffn_wd[l].astype(BF16),
                     final_g.reshape(1, d), l == DEPTH - 1)
    return x
```

```python
import functools

import jax
import jax.numpy as jnp
import numpy as np
from jax import lax
from jax.experimental import pallas as pl
from jax.experimental.pallas import tpu as pltpu

F32 = jnp.float32
BF16 = jnp.bfloat16

D_MODEL = 1024
DEPTH = 2
GLA_HEADS, GLA_DK, GLA_DV, GLA_RANK = 4, 32, 64, 16
GLA_GATE_NORM = 16.0
RET_HEADS, RET_DK, RET_DV = 4, 64, 64
ROPE_BASE = 10000.0
SB_HEADS, SB_DK, SB_DV = 8, 64, 64
D_FF = 2816
EPS = 1e-6

GLA_QK = GLA_HEADS * GLA_DK
GLA_V = GLA_HEADS * GLA_DV
RET_W = RET_HEADS * RET_DK
SB_W = SB_HEADS * SB_DK
C_GQ, C_GK, C_GV, C_GG = 0, 128, 256, 512
C_RQ, C_RK, C_RV, C_RG = 768, 1024, 1280, 1536
C_SQ, C_SK, C_SV = 1792, 2304, 2816
C_GR = 3328
IN_COLS_PADDED = 3456

LANES = 128
GLA_CHUNK = 64
GLA_LEVELS = 6
GLA_BLOCK = 256
RET_CHUNK = 256
SB_TQ = 512
SB_TK = 256
SB_UNDERFLOW_LOG2 = 160.0
TOKEN_TILE = 512
FF_CHUNK = 256
VMEM_LIMIT = 56 * 1024 * 1024


def _dot(a, b):
    return jnp.dot(a, b, preferred_element_type=F32)


def _dot_nt(a, b):
    return lax.dot_general(a, b, (((1,), (1,)), ((), ())), preferred_element_type=F32)


def _dot_tn(a, b):
    return lax.dot_general(a, b, (((0,), (0,)), ((), ())), preferred_element_type=F32)


def _sigmoid(x):
    return 1.0 / (1.0 + jnp.exp(-x))


def _softplus(x):
    return jnp.maximum(x, 0.0) + jnp.log(1.0 + jnp.exp(-jnp.abs(x)))


def _log2e(width):
    return 1.0 / jnp.log(jnp.full((1, width), 2.0, F32))


def _params(*sem):
    return pltpu.CompilerParams(dimension_semantics=sem, vmem_limit_bytes=VMEM_LIMIT)


def _mod_kernel(c_ref, w_ref, b_ref, o_ref):
    c = c_ref[...]
    cs = c * _sigmoid(c)
    o_ref[0] = jnp.dot(cs, w_ref[0], preferred_element_type=F32,
                       precision=lax.Precision.HIGHEST) + b_ref[0]


def _modulation(c, ada_w, ada_b):
    depth, d, n = ada_w.shape
    b = c.shape[0]
    tn = 1536
    return pl.pallas_call(
        _mod_kernel,
        grid=(depth, n // tn),
        in_specs=[pl.BlockSpec((b, d), lambda l, j: (0, 0)),
                  pl.BlockSpec((1, d, tn), lambda l, j: (l, 0, j)),
                  pl.BlockSpec((1, 1, tn), lambda l, j: (l, 0, j))],
        out_specs=pl.BlockSpec((1, b, tn), lambda l, j: (l, 0, j)),
        out_shape=jax.ShapeDtypeStruct((depth, b, n), F32),
        compiler_params=_params("arbitrary", "arbitrary"),
        name="adaln_modulation",
    )(c, ada_w, ada_b.reshape(depth, 1, n))


def _rope(t, cos, sin):
    lane = lax.broadcasted_iota(jnp.int32, cos.shape, 1)
    even = (lane & 1) == 0
    outs = []
    for c in range(t.shape[1] // LANES):
        tc = t[:, c * LANES:(c + 1) * LANES]
        nxt = pltpu.roll(tc, LANES - 1, 1)
        prv = pltpu.roll(tc, 1, 1)
        outs.append(tc * cos + jnp.where(even, nxt, prv) * sin)
    return jnp.concatenate(outs, axis=1)


def _inproj_kernel(x_ref, mod_ref, g_ref, w_ref, wa2_ref, ba_ref, cos_ref, sin_ref,
                   gq_ref, gk_ref, gv_ref, gg_ref, la_ref,
                   rq_ref, rk_ref, rv_ref, rg_ref, sq_ref, skT_ref, sv_ref):
    d = D_MODEL
    x = x_ref[0]
    mod = mod_ref[0]
    sh1, sc1 = mod[:, 0:d], mod[:, d:2 * d]
    ms = jnp.mean(x * x, axis=-1, keepdims=True)
    h = (x * lax.rsqrt(ms + EPS) * (g_ref[...] * (1.0 + sc1)) + sh1).astype(BF16)

    def proj(a, width):
        return _dot(h, w_ref[:, a:a + width])

    gq_ref[0] = proj(C_GQ, GLA_QK) * (GLA_DK ** -0.5)
    gk_ref[0] = proj(C_GK, GLA_QK)
    gv_ref[0] = proj(C_GV, GLA_V)
    gg_ref[0] = proj(C_GG, GLA_V)
    u = _dot(proj(C_GR, LANES).astype(BF16), wa2_ref[...]) + ba_ref[...]
    la_ref[0] = -_softplus(-u) * (1.0 / GLA_GATE_NORM)

    cos, sin = cos_ref[...], sin_ref[...]
    rq_ref[0] = _rope(proj(C_RQ, RET_W), cos, sin)
    rk_ref[0] = _rope(proj(C_RK, RET_W), cos, sin) * (RET_DK ** -0.5)
    rv_ref[0] = proj(C_RV, RET_W)
    rg_ref[0] = proj(C_RG, RET_W)

    sq_ref[0] = (proj(C_SQ, SB_W) * (_log2e(SB_W) * SB_DK ** -0.5)).astype(BF16)
    skT_ref[0] = proj(C_SK, SB_W).T.astype(BF16)
    sv_ref[0] = proj(C_SV, SB_W).astype(BF16)


def _input_projection(x, mod, g, w, wa2, ba, cos, sin):
    b, s, d = x.shape
    tm = min(TOKEN_TILE, s)
    tok = lambda width: pl.BlockSpec((1, tm, width), lambda bi, i: (bi, i, 0))
    tokT = lambda width: pl.BlockSpec((1, width, tm), lambda bi, i: (bi, 0, i))
    const = lambda shape: pl.BlockSpec(shape, lambda bi, i: tuple(0 for _ in shape))
    f32 = lambda width: jax.ShapeDtypeStruct((b, s, width), F32)
    return pl.pallas_call(
        _inproj_kernel,
        grid=(b, s // tm),
        in_specs=[tok(d),
                  pl.BlockSpec((1, 1, 6 * d), lambda bi, i: (bi, 0, 0)),
                  const((1, d)), const((d, IN_COLS_PADDED)), const((LANES, GLA_QK)), const((1, GLA_QK)),
                  pl.BlockSpec((tm, LANES), lambda bi, i: (i, 0)),
                  pl.BlockSpec((tm, LANES), lambda bi, i: (i, 0))],
        out_specs=[tok(GLA_QK), tok(GLA_QK), tok(GLA_V), tok(GLA_V), tok(GLA_QK),
                   tok(RET_W), tok(RET_W), tok(RET_W), tok(RET_W),
                   tok(SB_W), tokT(SB_W), tok(SB_W)],
        out_shape=[f32(GLA_QK), f32(GLA_QK), f32(GLA_V), f32(GLA_V), f32(GLA_QK),
                   f32(RET_W), f32(RET_W), f32(RET_W), f32(RET_W),
                   jax.ShapeDtypeStruct((b, s, SB_W), BF16),
                   jax.ShapeDtypeStruct((b, SB_W, s), BF16),
                   jax.ShapeDtypeStruct((b, s, SB_W), BF16)],
        compiler_params=_params("arbitrary", "arbitrary"),
        name="input_projection",
    )(x, mod, g, w, wa2, ba, cos, sin)


def _gla_tables():
    c, nl = GLA_CHUNK, GLA_LEVELS
    t = np.arange(c)
    rows = []
    for l in range(nl):
        w = 1 << l
        start = (t // w) * w
        rows.append((t[None, :] >= start[:, None]) & (t[None, :] <= t[:, None]))
    for l in range(nl):
        w = 1 << l
        end = (t // w + 1) * w - 1
        rows.append((t[None, :] > t[:, None]) & (t[None, :] <= end[:, None]))
    rows.append(t[None, :] <= t[:, None])
    mcat = np.concatenate(rows, 0).astype(np.float32)
    mcat3 = np.concatenate([mcat] * 3, 1)
    masks = []
    for l in range(nl):
        w = 1 << l
        same = (t[:, None] // (2 * w)) == (t[None, :] // (2 * w))
        upper = ((t // w) % 2 == 1)[:, None]
        lower = ((t // w) % 2 == 0)[None, :]
        masks.append(np.tile(same & upper & lower, (1, GLA_HEADS)))
    masks.append(np.tile(np.eye(c, dtype=bool), (1, GLA_HEADS)))
    lmask = np.stack(masks).astype(np.float32)
    r = np.arange(GLA_HEADS * c)
    kmask = (r[:, None] // c == np.arange(GLA_QK)[None, :] // GLA_DK).astype(np.float32)
    vmask = (r[:, None] // c == np.arange(GLA_V)[None, :] // GLA_DV).astype(np.float32)
    smask = (np.arange(GLA_V)[:, None] // GLA_DV == np.arange(GLA_QK)[None, :] // GLA_DK).astype(np.float32)
    return (jnp.asarray(mcat3, BF16), jnp.asarray(lmask), jnp.asarray(kmask),
            jnp.asarray(vmask), jnp.asarray(smask))


def _split3(x):
    hi = x.astype(BF16)
    r1 = x - hi.astype(F32)
    mid = r1.astype(BF16)
    lo = (r1 - mid.astype(F32)).astype(BF16)
    return hi, mid, lo


def _gla_kernel(q_ref, k_ref, v_ref, la_ref, mcat_ref, lmask_ref, kmask_ref, vmask_ref, smask_ref,
                o_ref, st_ref, *, nchunk):
    c, nl = GLA_CHUNK, GLA_LEVELS

    @pl.when(pl.program_id(1) == 0)
    def _():
        st_ref[...] = jnp.zeros_like(st_ref)

    mcat = mcat_ref[...]
    kmask, vmask, smask = kmask_ref[...], vmask_ref[...], smask_ref[...]
    for ci in range(nchunk):
        sl = slice(ci * c, (ci + 1) * c)
        q, k, v, la = q_ref[0, sl, :], k_ref[0, sl, :], v_ref[0, sl, :], la_ref[0, sl, :]
        cs = _dot(mcat, jnp.concatenate(_split3(la), axis=0))
        e = jnp.exp(cs)
        scores = jnp.zeros((c, GLA_HEADS * c), F32)
        for l in range(nl + 1):
            if l < nl:
                qt = q * e[l * c:(l + 1) * c]
                kt = k * e[(nl + l) * c:(nl + l + 1) * c]
            else:
                qt, kt = q, k
            krows = (jnp.concatenate([kt] * GLA_HEADS, axis=0) * kmask).astype(BF16)
            scores = scores + _dot_nt(qt.astype(BF16), krows) * lmask_ref[l]
        vbd = (jnp.concatenate([v] * GLA_HEADS, axis=0) * vmask).astype(BF16)
        o = _dot(scores.astype(BF16), vbd)
        bcum = cs[2 * nl * c:(2 * nl + 1) * c]
        st = st_ref[...]
        o = o + _dot_nt((q * e[2 * nl * c:(2 * nl + 1) * c]).astype(BF16), st.astype(BF16))
        blast = bcum[c - 1:c, :]
        kd = k * jnp.exp(blast - bcum)
        st_ref[...] = st * jnp.exp(blast) + _dot_tn(v.astype(BF16), kd.astype(BF16)) * smask
        o_ref[0, sl, :] = o


def _gla(q, k, v, la, tables):
    b, s, _ = q.shape
    blk = min(GLA_BLOCK, s)
    tok = lambda width: pl.BlockSpec((1, blk, width), lambda bi, i: (bi, i, 0))
    const = lambda a: pl.BlockSpec(a.shape, lambda bi, i: tuple(0 for _ in a.shape))
    return pl.pallas_call(
        functools.partial(_gla_kernel, nchunk=blk // GLA_CHUNK),
        grid=(b, s // blk),
        in_specs=[tok(GLA_QK), tok(GLA_QK), tok(GLA_V), tok(GLA_QK)] + [const(a) for a in tables],
        out_specs=tok(GLA_V),
        out_shape=jax.ShapeDtypeStruct((b, s, GLA_V), F32),
        scratch_shapes=[pltpu.VMEM((GLA_V, GLA_QK), F32)],
        compiler_params=_params("arbitrary", "arbitrary"),
        name="gla_mixer",
    )(q, k, v, la, *tables)


def _ret_tables():
    c = RET_CHUNK
    gam = 1.0 - 2.0 ** (-5.0 - np.arange(RET_HEADS, dtype=np.float64))
    t = np.arange(c)
    rel = t[:, None] - t[None, :]
    dmat = np.where(rel[None] >= 0, gam[:, None, None] ** np.maximum(rel, 0)[None], 0.0)
    dall = np.concatenate(list(dmat), axis=1)
    xi = np.repeat((gam[None, :] ** (t[:, None] + 1.0)), RET_DK, axis=1)
    zeta = np.repeat((gam[None, :] ** (c - 1.0 - t[:, None])), RET_DK, axis=1)
    gc = np.repeat(gam ** c, RET_DK)[:, None] * np.ones((1, RET_W))
    r = np.arange(RET_HEADS * c)
    hmask = (r[:, None] // c == np.arange(RET_W)[None, :] // RET_DK)
    smask = (np.arange(RET_W)[:, None] // RET_DK == np.arange(RET_W)[None, :] // RET_DV)
    return tuple(jnp.asarray(a, F32) for a in (dall, xi, zeta, gc, hmask, smask))


def _ret_kernel(q_ref, k_ref, v_ref, dall_ref, xi_ref, zeta_ref, gc_ref, hmask_ref, smask_ref,
                o_ref, r_ref):
    @pl.when(pl.program_id(1) == 0)
    def _():
        r_ref[...] = jnp.zeros_like(r_ref)

    q, k, v = q_ref[0], k_ref[0], v_ref[0]
    hmask = hmask_ref[...]
    krows = (jnp.concatenate([k] * RET_HEADS, axis=0) * hmask).astype(BF16)
    scores = _dot_nt(q.astype(BF16), krows) * dall_ref[...]
    vbd = (jnp.concatenate([v] * RET_HEADS, axis=0) * hmask).astype(BF16)
    o = _dot(scores.astype(BF16), vbd)
    r = r_ref[...]
    o = o + _dot((q * xi_ref[...]).astype(BF16), r.astype(BF16))
    upd = _dot_tn((k * zeta_ref[...]).astype(BF16), v.astype(BF16))
    r_ref[...] = r * gc_ref[...] + upd * smask_ref[...]
    o_ref[0] = o


def _retention(q, k, v, tables):
    b, s, _ = q.shape
    c = RET_CHUNK
    tok = pl.BlockSpec((1, c, RET_W), lambda bi, i: (bi, i, 0))
    const = lambda a: pl.BlockSpec(a.shape, lambda bi, i: tuple(0 for _ in a.shape))
    return pl.pallas_call(
        _ret_kernel,
        grid=(b, s // c),
        in_specs=[tok, tok, tok] + [const(a) for a in tables],
        out_specs=tok,
        out_shape=jax.ShapeDtypeStruct((b, s, RET_W), F32),
        scratch_shapes=[pltpu.VMEM((RET_W, RET_W), F32)],
        compiler_params=_params("arbitrary", "arbitrary"),
        name="retention_mixer",
    )(q, k, v, *tables)


def _sb_kernel(q_ref, kT_ref, v_ref, w_ref, o_ref, acc_ref, c_ref):
    tq, tk, half = SB_TQ, SB_TK, LANES
    i = pl.program_id(2)
    q2 = q_ref[0]
    lane = lax.broadcasted_iota(jnp.int32, q2.shape, 1)
    zero = jnp.zeros_like(q2)
    qs = (jnp.where(lane < SB_DK, q2, zero), jnp.where(lane >= SB_DK, q2, zero))
    acc_ref[...] = jnp.zeros_like(acc_ref)
    c_ref[...] = jnp.zeros_like(c_ref)
    w = w_ref[...]
    log2e = _log2e(half)

    def tile(j, masked, r0=0):
        rows = slice(r0, tq)
        start = pl.multiple_of(j * tk, tk)
        kT2 = kT_ref[0, :, pl.ds(start, tk)]
        v2 = v_ref[0, pl.ds(start, tk), :]
        for hd in range(2):
            z = _dot(qs[hd][rows], kT2)
            c = c_ref[hd, rows, :]
            ps = [None] * (tk // half)
            for h in reversed(range(tk // half)):
                zh = z[:, h * half:(h + 1) * half]
                nabs = pltpu.bitcast(pltpu.bitcast(zh, jnp.uint32) | jnp.uint32(0x80000000), F32)
                sp = jnp.log(1.0 + jnp.exp2(nabs)) * log2e + jnp.maximum(zh, 0.0)
                if masked:
                    key_s = lax.broadcasted_iota(jnp.int32, zh.shape, 1) + (start + h * half)
                    query_t = lax.broadcasted_iota(jnp.int32, zh.shape, 0) + (i * tq + r0)
                    causal = key_s < query_t
                    sp = jnp.where(causal, sp, 0.0)
                hi = sp.astype(BF16)
                lo = (sp - hi.astype(F32)).astype(BF16)
                out = _dot(jnp.concatenate([hi, lo], axis=1), w)
                p = jnp.exp2(zh - (out[:, :half] + c))
                c = c + out[:, half:]
                if masked:
                    p = jnp.where(causal, p, 0.0)
                ps[h] = p.astype(BF16)
            c_ref[hd, rows, :] = c
            acc_ref[hd, rows, :] += _dot(jnp.concatenate(ps, axis=1), v2)

    nk = tq // tk
    for d in reversed(range(nk)):
        tile(i * nk + d, True, r0=d * tk)

    def cond(st):
        return jnp.logical_and(st[0] < i * nk, st[1] < SB_UNDERFLOW_LOG2)

    def body(st):
        tile(i * nk - 1 - st[0], False)
        return st[0] + 1, jnp.min(c_ref[...])

    lax.while_loop(cond, body, (jnp.int32(0), jnp.min(c_ref[...])))
    o_ref[0] = jnp.where(lane < SB_DK, acc_ref[0], acc_ref[1]).astype(BF16)


def _stick_breaking(q, kT, v):
    b, s, w = q.shape
    tq = SB_TQ
    pairs = w // LANES
    r = np.arange(LANES)
    tri_ones = np.concatenate([r[:, None] >= r[None, :], np.ones((LANES, LANES), bool)], axis=1)
    wmat = jnp.asarray(np.concatenate([tri_ones, tri_ones], axis=0), BF16)
    return pl.pallas_call(
        _sb_kernel,
        grid=(b, pairs, s // tq),
        in_specs=[pl.BlockSpec((1, tq, LANES), lambda bi, p, i: (bi, i, p)),
                  pl.BlockSpec((1, LANES, s), lambda bi, p, i: (bi, p, 0)),
                  pl.BlockSpec((1, s, LANES), lambda bi, p, i: (bi, 0, p)),
                  pl.BlockSpec(wmat.shape, lambda bi, p, i: (0, 0))],
        out_specs=pl.BlockSpec((1, tq, LANES), lambda bi, p, i: (bi, i, p)),
        out_shape=jax.ShapeDtypeStruct((b, s, w), BF16),
        scratch_shapes=[pltpu.VMEM((2, tq, LANES), F32), pltpu.VMEM((2, tq, LANES), F32)],
        compiler_params=_params("arbitrary", "arbitrary", "arbitrary"),
        name="stick_breaking_mixer",
    )(q, kT, v, wmat)


def _head_norm_gate(o, g, gate, bd):
    sq = o * o
    hi = sq.astype(BF16)
    lo = (sq - hi.astype(F32)).astype(BF16)
    ms = _dot(hi, bd) + _dot(lo, bd)
    return o * lax.rsqrt(ms + EPS) * g * (gate * _sigmoid(gate))


def _ffn_kernel(x_ref, mod_ref, og_ref, gg_ref, or_ref, rg_ref, osb_ref, gng_ref, rng_ref, bd_ref,
                wout_ref, n2g_ref, wg_ref, wu_ref, wd_ref, fg_ref, o_ref, p_ref, *, final):
    d = D_MODEL
    x = x_ref[0]
    mod = mod_ref[0]
    g1, sh2, sc2, g2 = mod[:, 2 * d:3 * d], mod[:, 3 * d:4 * d], mod[:, 4 * d:5 * d], mod[:, 5 * d:6 * d]
    bd = bd_ref[...]
    o_gla = _head_norm_gate(og_ref[0], gng_ref[...], gg_ref[0], bd)
    o_ret = _head_norm_gate(or_ref[0], rng_ref[...], rg_ref[0], bd)
    o = jnp.concatenate([o_gla.astype(BF16), o_ret.astype(BF16), osb_ref[0]], axis=1)
    x1 = x + g1 * _dot(o, wout_ref[...])
    ms = jnp.mean(x1 * x1, axis=-1, keepdims=True)
    h = (x1 * lax.rsqrt(ms + EPS) * (n2g_ref[...] * (1.0 + sc2)) + sh2).astype(BF16)
    for c0 in range(0, D_FF, FF_CHUNK):
        a = _dot(h, wg_ref[:, c0:c0 + FF_CHUNK])
        u = _dot(h, wu_ref[:, c0:c0 + FF_CHUNK])
        p_ref[:, c0:c0 + FF_CHUNK] = (a * _sigmoid(a) * u).astype(BF16)
    x2 = x1 + g2 * _dot(p_ref[...], wd_ref[...])
    if final:
        ms = jnp.mean(x2 * x2, axis=-1, keepdims=True)
        x2 = x2 * lax.rsqrt(ms + EPS) * fg_ref[...]
    o_ref[0] = x2


def _out_ffn(x, mod, og, gg, orr, rg, osb, gng, rng, bd, wout, n2g, wg, wu, wd, fg, final):
    b, s, d = x.shape
    tm = min(TOKEN_TILE, s)
    tok = lambda width: pl.BlockSpec((1, tm, width), lambda bi, i: (bi, i, 0))
    const = lambda a: pl.BlockSpec(a.shape, lambda bi, i: tuple(0 for _ in a.shape),
                                   pipeline_mode=pl.Buffered(1))
    return pl.pallas_call(
        functools.partial(_ffn_kernel, final=final),
        grid=(b, s // tm),
        in_specs=[tok(d), pl.BlockSpec((1, 1, 6 * d), lambda bi, i: (bi, 0, 0)),
                  tok(GLA_V), tok(GLA_V), tok(RET_W), tok(RET_W), tok(SB_W),
                  const(gng), const(rng), const(bd), const(wout), const(n2g),
                  const(wg), const(wu), const(wd), const(fg)],
        out_specs=tok(d),
        out_shape=jax.ShapeDtypeStruct((b, s, d), F32),
        scratch_shapes=[pltpu.VMEM((tm, D_FF), BF16)],
        compiler_params=_params("arbitrary", "arbitrary"),
        name="out_projection_ffn",
    )(x, mod, og, gg, orr, rg, osb, gng, rng, bd, wout, n2g, wg, wu, wd, fg)


def _rope_tables(s):
    inv = ROPE_BASE ** (-np.arange(0, RET_DK, 2, dtype=np.float64) / RET_DK)
    ang = np.arange(s, dtype=np.float64)[:, None] * inv[None, :]
    cos = np.repeat(np.cos(ang), 2, axis=1)
    sin = np.repeat(np.sin(ang), 2, axis=1) * np.tile([-1.0, 1.0], RET_DK // 2)[None, :]
    reps = LANES // RET_DK
    return jnp.asarray(np.tile(cos, (1, reps)), F32), jnp.asarray(np.tile(sin, (1, reps)), F32)


def kernel(x, c, ada_w, ada_b, norm1_g, norm2_g, w_in, gla_wa2, gla_ba, gla_norm_g, ret_norm_g,
           w_out, ffn_wg, ffn_wu, ffn_wd, final_g):
    b, s, d = x.shape
    mods = _modulation(c, ada_w, ada_b)
    cos, sin = _rope_tables(s)
    gla_tables = _gla_tables()
    ret_tables = _ret_tables()
    hd = np.arange(GLA_V)
    bd = jnp.asarray((hd[:, None] // GLA_DV == hd[None, :] // GLA_DV) / float(GLA_DV), BF16)
    gr0 = C_RQ
    for l in range(DEPTH):
        w = w_in[l]
        w_perm = jnp.concatenate(
            [w[:, :gr0], w[:, gr0 + GLA_RANK:], w[:, gr0:gr0 + GLA_RANK],
             jnp.zeros((d, IN_COLS_PADDED - C_GR - GLA_RANK), w.dtype)], axis=1).astype(BF16)
        wa2 = jnp.concatenate([gla_wa2[l], jnp.zeros((LANES - GLA_RANK, GLA_QK), F32)], axis=0).astype(BF16)
        mod = mods[l].reshape(b, 1, 6 * d)
        (gq, gk, gv, gg, la, rq, rk, rv, rg, sq, skT, sv) = _input_projection(
            x, mod, norm1_g[l].reshape(1, d), w_perm, wa2, gla_ba[l].reshape(1, GLA_QK), cos, sin)
        o_gla = _gla(gq, gk, gv, la, gla_tables)
        o_ret = _retention(rq, rk, rv, ret_tables)
        o_sb = _stick_breaking(sq, skT, sv)
        x = _out_ffn(x, mod, o_gla, gg, o_ret, rg, o_sb,
                     jnp.tile(gla_norm_g[l], GLA_HEADS).reshape(1, GLA_V),
                     jnp.tile(ret_norm_g[l], RET_HEADS).reshape(1, RET_W), bd,
                     w_out[l].astype(BF16), norm2_g[l].reshape(1, d),
                     ffn_wg[l].astype(BF16), ffn_wu[l].astype(BF16), ffn_wd[l].astype(BF16),
                     final_g.reshape(1, d), l == DEPTH - 1)
    return x
```

```python
import functools

import jax
import jax.numpy as jnp
import numpy as np
from jax import lax
from jax.experimental import pallas as pl
from jax.experimental.pallas import tpu as pltpu

F32 = jnp.float32
BF16 = jnp.bfloat16

D_MODEL = 1024
DEPTH = 2
GLA_HEADS, GLA_DK, GLA_DV, GLA_RANK = 4, 32, 64, 16
GLA_GATE_NORM = 16.0
RET_HEADS, RET_DK, RET_DV = 4, 64, 64
ROPE_BASE = 10000.0
SB_HEADS, SB_DK, SB_DV = 8, 64, 64
D_FF = 2816
EPS = 1e-6

GLA_QK = GLA_HEADS * GLA_DK
GLA_V = GLA_HEADS * GLA_DV
RET_W = RET_HEADS * RET_DK
SB_W = SB_HEADS * SB_DK
C_GQ, C_GK, C_GV, C_GG = 0, 128, 256, 512
C_RQ, C_RK, C_RV, C_RG = 768, 1024, 1280, 1536
C_SQ, C_SK, C_SV = 1792, 2304, 2816
C_GR = 3328
IN_COLS_PADDED = 3456

LANES = 128
GLA_CHUNK = 64
GLA_LEVELS = 6
GLA_BLOCK = 512
RET_CHUNK = 256
SB_TQ = 512
SB_TK = 256
SB_UNDERFLOW_LOG2 = 160.0
SB_EXP_CLAMP = 64.0
TOKEN_TILE = 512
FF_CHUNK = 256
VMEM_LIMIT = 56 * 1024 * 1024


def _dot(a, b):
    return jnp.dot(a, b, preferred_element_type=F32)


def _dot_nt(a, b):
    return lax.dot_general(a, b, (((1,), (1,)), ((), ())), preferred_element_type=F32)


def _dot_tn(a, b):
    return lax.dot_general(a, b, (((0,), (0,)), ((), ())), preferred_element_type=F32)


def _sigmoid(x):
    return 1.0 / (1.0 + jnp.exp(-x))


def _softplus(x):
    return jnp.maximum(x, 0.0) + jnp.log(1.0 + jnp.exp(-jnp.abs(x)))


def _log2e(width):
    return 1.0 / jnp.log(jnp.full((1, width), 2.0, F32))


def _params(*sem):
    return pltpu.CompilerParams(dimension_semantics=sem, vmem_limit_bytes=VMEM_LIMIT)


def _mod_kernel(c_ref, w_ref, b_ref, o_ref):
    c = c_ref[...]
    cs = c * _sigmoid(c)
    o_ref[0] = jnp.dot(cs, w_ref[0], preferred_element_type=F32,
                       precision=lax.Precision.HIGHEST) + b_ref[0]


def _modulation(c, ada_w, ada_b):
    depth, d, n = ada_w.shape
    b = c.shape[0]
    tn = 1536
    return pl.pallas_call(
        _mod_kernel,
        grid=(depth, n // tn),
        in_specs=[pl.BlockSpec((b, d), lambda l, j: (0, 0)),
                  pl.BlockSpec((1, d, tn), lambda l, j: (l, 0, j)),
                  pl.BlockSpec((1, 1, tn), lambda l, j: (l, 0, j))],
        out_specs=pl.BlockSpec((1, b, tn), lambda l, j: (l, 0, j)),
        out_shape=jax.ShapeDtypeStruct((depth, b, n), F32),
        compiler_params=_params("arbitrary", "arbitrary"),
        name="adaln_modulation",
    )(c, ada_w, ada_b.reshape(depth, 1, n))


def _rope(t, cos, sin):
    lane = lax.broadcasted_iota(jnp.int32, cos.shape, 1)
    even = (lane & 1) == 0
    outs = []
    for c in range(t.shape[1] // LANES):
        tc = t[:, c * LANES:(c + 1) * LANES]
        nxt = pltpu.roll(tc, LANES - 1, 1)
        prv = pltpu.roll(tc, 1, 1)
        outs.append(tc * cos + jnp.where(even, nxt, prv) * sin)
    return jnp.concatenate(outs, axis=1)


def _inproj_kernel(x_ref, mod_ref, g_ref, w_ref, wa2_ref, ba_ref, cos_ref, sin_ref,
                   gq_ref, gk_ref, gv_ref, gg_ref, la_ref,
                   rq_ref, rk_ref, rv_ref, rg_ref, sq_ref, skT_ref, sv_ref):
    d = D_MODEL
    x = x_ref[0]
    mod = mod_ref[0]
    sh1, sc1 = mod[:, 0:d], mod[:, d:2 * d]
    ms = jnp.mean(x * x, axis=-1, keepdims=True)
    h = (x * lax.rsqrt(ms + EPS) * (g_ref[...] * (1.0 + sc1)) + sh1).astype(BF16)

    def proj(a, width):
        return _dot(h, w_ref[:, a:a + width])

    gqk = proj(C_GQ, 2 * GLA_QK)
    gq_ref[0] = gqk[:, :GLA_QK] * (GLA_DK ** -0.5)
    gk_ref[0] = gqk[:, GLA_QK:]
    gv_ref[0] = proj(C_GV, GLA_V)
    gg_ref[0] = proj(C_GG, GLA_V)
    sv_head = proj(C_SV, SB_W - LANES)
    sv_tail_gr = proj(C_GR - LANES, 2 * LANES)
    u = _dot(sv_tail_gr[:, LANES:].astype(BF16), wa2_ref[...]) + ba_ref[...]
    la_ref[0] = -_softplus(-u) * (1.0 / GLA_GATE_NORM)

    cos, sin = cos_ref[...], sin_ref[...]
    rq_ref[0] = _rope(proj(C_RQ, RET_W), cos, sin)
    rk_ref[0] = _rope(proj(C_RK, RET_W), cos, sin) * (RET_DK ** -0.5)
    rv_ref[0] = proj(C_RV, RET_W)
    rg_ref[0] = proj(C_RG, RET_W)

    sq_ref[0] = (proj(C_SQ, SB_W) * (_log2e(SB_W) * SB_DK ** -0.5)).astype(BF16)
    skT_ref[0] = proj(C_SK, SB_W).T.astype(BF16)
    sv_ref[0] = jnp.concatenate([sv_head, sv_tail_gr[:, :LANES]], axis=1).astype(BF16)


def _input_projection(x, mod, g, w, wa2, ba, cos, sin):
    b, s, d = x.shape
    tm = min(TOKEN_TILE, s)
    tok = lambda width: pl.BlockSpec((1, tm, width), lambda bi, i: (bi, i, 0))
    tokT = lambda width: pl.BlockSpec((1, width, tm), lambda bi, i: (bi, 0, i))
    const = lambda shape: pl.BlockSpec(shape, lambda bi, i: tuple(0 for _ in shape))
    f32 = lambda width: jax.ShapeDtypeStruct((b, s, width), F32)
    return pl.pallas_call(
        _inproj_kernel,
        grid=(b, s // tm),
        in_specs=[tok(d),
                  pl.BlockSpec((1, 1, 6 * d), lambda bi, i: (bi, 0, 0)),
                  const((1, d)), const((d, IN_COLS_PADDED)), const((LANES, GLA_QK)), const((1, GLA_QK)),
                  pl.BlockSpec((tm, LANES), lambda bi, i: (i, 0)),
                  pl.BlockSpec((tm, LANES), lambda bi, i: (i, 0))],
        out_specs=[tok(GLA_QK), tok(GLA_QK), tok(GLA_V), tok(GLA_V), tok(GLA_QK),
                   tok(RET_W), tok(RET_W), tok(RET_W), tok(RET_W),
                   tok(SB_W), tokT(SB_W), tok(SB_W)],
        out_shape=[f32(GLA_QK), f32(GLA_QK), f32(GLA_V), f32(GLA_V), f32(GLA_QK),
                   f32(RET_W), f32(RET_W), f32(RET_W), f32(RET_W),
                   jax.ShapeDtypeStruct((b, s, SB_W), BF16),
                   jax.ShapeDtypeStruct((b, SB_W, s), BF16),
                   jax.ShapeDtypeStruct((b, s, SB_W), BF16)],
        compiler_params=_params("arbitrary", "arbitrary"),
        name="input_projection",
    )(x, mod, g, w, wa2, ba, cos, sin)


def _gla_tables():
    c, nl = GLA_CHUNK, GLA_LEVELS
    t = np.arange(c)
    rows = []
    for l in range(nl):
        w = 1 << l
        start = (t // w) * w
        rows.append((t[None, :] >= start[:, None]) & (t[None, :] <= t[:, None]))
    for l in range(nl):
        w = 1 << l
        end = (t // w + 1) * w - 1
        rows.append((t[None, :] > t[:, None]) & (t[None, :] <= end[:, None]))
    rows.append(t[None, :] <= t[:, None])
    mcat = np.concatenate(rows, 0).astype(np.float32)
    mcat3 = np.concatenate([mcat] * 3, 1)
    masks = []
    for l in range(nl):
        w = 1 << l
        same = (t[:, None] // (2 * w)) == (t[None, :] // (2 * w))
        upper = ((t // w) % 2 == 1)[:, None]
        lower = ((t // w) % 2 == 0)[None, :]
        masks.append(np.tile(same & upper & lower, (1, GLA_HEADS)))
    masks.append(np.tile(np.eye(c, dtype=bool), (1, GLA_HEADS)))
    lmask = np.stack(masks).astype(np.float32)
    r = np.arange(GLA_HEADS * c)
    kmask = (r[:, None] // c == np.arange(GLA_QK)[None, :] // GLA_DK).astype(np.float32)
    vmask = (r[:, None] // c == np.arange(GLA_V)[None, :] // GLA_DV).astype(np.float32)
    smask = (np.arange(GLA_V)[:, None] // GLA_DV == np.arange(GLA_QK)[None, :] // GLA_DK).astype(np.float32)
    return (jnp.asarray(mcat3, BF16), jnp.asarray(lmask), jnp.asarray(kmask),
            jnp.asarray(vmask), jnp.asarray(smask))


def _split3(x):
    hi = x.astype(BF16)
    r1 = x - hi.astype(F32)
    mid = r1.astype(BF16)
    lo = (r1 - mid.astype(F32)).astype(BF16)
    return hi, mid, lo


def _gla_kernel(q_ref, k_ref, v_ref, la_ref, mcat_ref, lmask_ref, kmask_ref, vmask_ref, smask_ref,
                o_ref, st_ref, *, nchunk):
    c, nl = GLA_CHUNK, GLA_LEVELS

    @pl.when(pl.program_id(1) == 0)
    def _():
        st_ref[...] = jnp.zeros_like(st_ref)

    kmask, vmask, smask = kmask_ref[...], vmask_ref[...], smask_ref[...]
    la3 = jnp.concatenate([jnp.concatenate(_split3(la_ref[0, ci * c:(ci + 1) * c, :]), axis=0)
                           for ci in range(nchunk)], axis=1)
    cs_all = _dot(mcat_ref[...], la3)
    e_all = jnp.exp(cs_all)
    for ci in range(nchunk):
        sl = slice(ci * c, (ci + 1) * c)
        q, k, v = q_ref[0, sl, :], k_ref[0, sl, :], v_ref[0, sl, :]
        cs = cs_all[:, ci * GLA_QK:(ci + 1) * GLA_QK]
        e = e_all[:, ci * GLA_QK:(ci + 1) * GLA_QK]
        scores = jnp.zeros((c, GLA_HEADS * c), F32)
        for l in range(nl + 1):
            if l < nl:
                qt = q * e[l * c:(l + 1) * c]
                kt = k * e[(nl + l) * c:(nl + l + 1) * c]
            else:
                qt, kt = q, k
            krows = (jnp.concatenate([kt] * GLA_HEADS, axis=0) * kmask).astype(BF16)
            scores = scores + _dot_nt(qt.astype(BF16), krows) * lmask_ref[l]
        vbd = (jnp.concatenate([v] * GLA_HEADS, axis=0) * vmask).astype(BF16)
        o = _dot(scores.astype(BF16), vbd)
        bcum = cs[2 * nl * c:(2 * nl + 1) * c]
        st = st_ref[...]
        o = o + _dot_nt((q * e[2 * nl * c:(2 * nl + 1) * c]).astype(BF16), st.astype(BF16))
        blast = bcum[c - 1:c, :]
        kd = k * jnp.exp(blast - bcum)
        st_ref[...] = st * jnp.exp(blast) + _dot_tn(v.astype(BF16), kd.astype(BF16)) * smask
        o_ref[0, sl, :] = o


def _gla(q, k, v, la, tables):
    b, s, _ = q.shape
    blk = min(GLA_BLOCK, s)
    tok = lambda width: pl.BlockSpec((1, blk, width), lambda bi, i: (bi, i, 0))
    const = lambda a: pl.BlockSpec(a.shape, lambda bi, i: tuple(0 for _ in a.shape))
    return pl.pallas_call(
        functools.partial(_gla_kernel, nchunk=blk // GLA_CHUNK),
        grid=(b, s // blk),
        in_specs=[tok(GLA_QK), tok(GLA_QK), tok(GLA_V), tok(GLA_QK)] + [const(a) for a in tables],
        out_specs=tok(GLA_V),
        out_shape=jax.ShapeDtypeStruct((b, s, GLA_V), F32),
        scratch_shapes=[pltpu.VMEM((GLA_V, GLA_QK), F32)],
        compiler_params=_params("arbitrary", "arbitrary"),
        name="gla_mixer",
    )(q, k, v, la, *tables)


def _ret_tables():
    c = RET_CHUNK
    gam = 1.0 - 2.0 ** (-5.0 - np.arange(RET_HEADS, dtype=np.float64))
    t = np.arange(c)
    rel = t[:, None] - t[None, :]
    dmat = np.where(rel[None] >= 0, gam[:, None, None] ** np.maximum(rel, 0)[None], 0.0)
    dall = np.concatenate(list(dmat), axis=1)
    xi = np.repeat((gam[None, :] ** (t[:, None] + 1.0)), RET_DK, axis=1)
    zeta = np.repeat((gam[None, :] ** (c - 1.0 - t[:, None])), RET_DK, axis=1)
    gc = np.repeat(gam ** c, RET_DK)[:, None] * np.ones((1, RET_W))
    r = np.arange(RET_HEADS * c)
    hmask = (r[:, None] // c == np.arange(RET_W)[None, :] // RET_DK)
    smask = (np.arange(RET_W)[:, None] // RET_DK == np.arange(RET_W)[None, :] // RET_DV)
    return tuple(jnp.asarray(a, F32) for a in (dall, xi, zeta, gc, hmask, smask))


def _ret_kernel(q_ref, k_ref, v_ref, dall_ref, xi_ref, zeta_ref, gc_ref, hmask_ref, smask_ref,
                o_ref, r_ref):
    @pl.when(pl.program_id(1) == 0)
    def _():
        r_ref[...] = jnp.zeros_like(r_ref)

    q, k, v = q_ref[0], k_ref[0], v_ref[0]
    hmask = hmask_ref[...]
    krows = (jnp.concatenate([k] * RET_HEADS, axis=0) * hmask).astype(BF16)
    scores = _dot_nt(q.astype(BF16), krows) * dall_ref[...]
    vbd = (jnp.concatenate([v] * RET_HEADS, axis=0) * hmask).astype(BF16)
    o = _dot(scores.astype(BF16), vbd)
    r = r_ref[...]
    o = o + _dot((q * xi_ref[...]).astype(BF16), r.astype(BF16))
    upd = _dot_tn((k * zeta_ref[...]).astype(BF16), v.astype(BF16))
    r_ref[...] = r * gc_ref[...] + upd * smask_ref[...]
    o_ref[0] = o


def _retention(q, k, v, tables):
    b, s, _ = q.shape
    c = RET_CHUNK
    tok = pl.BlockSpec((1, c, RET_W), lambda bi, i: (bi, i, 0))
    const = lambda a: pl.BlockSpec(a.shape, lambda bi, i: tuple(0 for _ in a.shape))
    return pl.pallas_call(
        _ret_kernel,
        grid=(b, s // c),
        in_specs=[tok, tok, tok] + [const(a) for a in tables],
        out_specs=tok,
        out_shape=jax.ShapeDtypeStruct((b, s, RET_W), F32),
        scratch_shapes=[pltpu.VMEM((RET_W, RET_W), F32)],
        compiler_params=_params("arbitrary", "arbitrary"),
        name="retention_mixer",
    )(q, k, v, *tables)


def _sb_kernel(q_ref, kT_ref, v_ref, w_ref, o_ref, acc_ref, c_ref):
    tq, tk = SB_TQ, SB_TK
    i = pl.program_id(2)
    q2 = q_ref[0]
    lane = lax.broadcasted_iota(jnp.int32, q2.shape, 1)
    zero = jnp.zeros_like(q2)
    qs = (jnp.where(lane < SB_DK, q2, zero), jnp.where(lane >= SB_DK, q2, zero))
    acc_ref[...] = jnp.zeros_like(acc_ref)
    c_ref[...] = jnp.zeros_like(c_ref)
    w = w_ref[...]
    log2e = _log2e(tk)

    def tile(j, r0, r1, mask_rows):
        start = pl.multiple_of(j * tk, tk)
        kT2 = kT_ref[0, :, pl.ds(start, tk)]
        v2 = v_ref[0, pl.ds(start, tk), :]
        sections = [(r0, r0 + mask_rows, True), (r0 + mask_rows, r1, False)]
        for hd in range(2):
            z = _dot(qs[hd][r0:r1], kT2)
            for a, b, masked in sections:
                if a == b:
                    continue
                zs = z[a - r0:b - r0]
                c = c_ref[hd, a:b, :]
                sp = jnp.maximum(jnp.log(1.0 + jnp.exp2(jnp.minimum(zs, SB_EXP_CLAMP))) * log2e, zs)
                if masked:
                    key_s = lax.broadcasted_iota(jnp.int32, zs.shape, 1) + start
                    query_t = lax.broadcasted_iota(jnp.int32, zs.shape, 0) + (i * tq + a)
                    causal = key_s < query_t
                    sp = jnp.where(causal, sp, 0.0)
                r = _dot(sp.astype(BF16), w)
                p = jnp.exp2(zs - (r + jnp.concatenate([c] * (tk // LANES), axis=1)))
                if masked:
                    p = jnp.where(causal, p, 0.0)
                c_ref[hd, a:b, :] = c + jnp.broadcast_to(r[:, 0:1], c.shape)
                acc_ref[hd, a:b, :] += _dot(p.astype(BF16), v2)

    nk = tq // tk
    for d in reversed(range(nk)):
        tile(i * nk + d, d * tk, tq, tk)

    def mins():
        c = c_ref[...]
        return jnp.min(c[:, :tq // 2]), jnp.min(c[:, tq // 2:])

    def cond(st):
        return jnp.logical_and(st[0] < i * nk, jnp.minimum(st[1], st[2]) < SB_UNDERFLOW_LOG2)

    def body(st):
        j = i * nk - 1 - st[0]
        lax.cond(st[2] >= SB_UNDERFLOW_LOG2, lambda: tile(j, 0, tq // 2, 0), lambda: tile(j, 0, tq, 0))
        return (st[0] + 1,) + mins()

    lax.while_loop(cond, body, (jnp.int32(0),) + mins())
    o_ref[0] = jnp.where(lane < SB_DK, acc_ref[0], acc_ref[1]).astype(BF16)


def _stick_breaking(q, kT, v):
    b, s, w = q.shape
    tq = SB_TQ
    pairs = w // LANES
    r = np.arange(SB_TK)
    wmat = jnp.asarray(r[:, None] >= r[None, :], BF16)
    return pl.pallas_call(
        _sb_kernel,
        grid=(b, pairs, s // tq),
        in_specs=[pl.BlockSpec((1, tq, LANES), lambda bi, p, i: (bi, i, p)),
                  pl.BlockSpec((1, LANES, s), lambda bi, p, i: (bi, p, 0)),
                  pl.BlockSpec((1, s, LANES), lambda bi, p, i: (bi, 0, p)),
                  pl.BlockSpec(wmat.shape, lambda bi, p, i: (0, 0))],
        out_specs=pl.BlockSpec((1, tq, LANES), lambda bi, p, i: (bi, i, p)),
        out_shape=jax.ShapeDtypeStruct((b, s, w), BF16),
        scratch_shapes=[pltpu.VMEM((2, tq, LANES), F32), pltpu.VMEM((2, tq, LANES), F32)],
        compiler_params=_params("arbitrary", "arbitrary", "arbitrary"),
        name="stick_breaking_mixer",
    )(q, kT, v, wmat)


def _head_norm_gate(o, g, gate, bd):
    sq = o * o
    hi = sq.astype(BF16)
    lo = (sq - hi.astype(F32)).astype(BF16)
    ms = _dot(hi, bd) + _dot(lo, bd)
    return o * lax.rsqrt(ms + EPS) * g * (gate * _sigmoid(gate))


def _ffn_kernel(x_ref, mod_ref, og_ref, gg_ref, or_ref, rg_ref, osb_ref, gng_ref, rng_ref, bd_ref,
                wout_ref, n2g_ref, wg_ref, wu_ref, wd_ref, fg_ref, o_ref, p_ref, *, final):
    d = D_MODEL
    x = x_ref[0]
    mod = mod_ref[0]
    g1, sh2, sc2, g2 = mod[:, 2 * d:3 * d], mod[:, 3 * d:4 * d], mod[:, 4 * d:5 * d], mod[:, 5 * d:6 * d]
    bd = bd_ref[...]
    o_gla = _head_norm_gate(og_ref[0], gng_ref[...], gg_ref[0], bd)
    o_ret = _head_norm_gate(or_ref[0], rng_ref[...], rg_ref[0], bd)
    o = jnp.concatenate([o_gla.astype(BF16), o_ret.astype(BF16), osb_ref[0]], axis=1)
    x1 = x + g1 * _dot(o, wout_ref[...])
    ms = jnp.mean(x1 * x1, axis=-1, keepdims=True)
    h = (x1 * lax.rsqrt(ms + EPS) * (n2g_ref[...] * (1.0 + sc2)) + sh2).astype(BF16)
    for c0 in range(0, D_FF, FF_CHUNK):
        a = _dot(h, wg_ref[:, c0:c0 + FF_CHUNK])
        u = _dot(h, wu_ref[:, c0:c0 + FF_CHUNK])
        p_ref[:, c0:c0 + FF_CHUNK] = (a * _sigmoid(a) * u).astype(BF16)
    x2 = x1 + g2 * _dot(p_ref[...], wd_ref[...])
    if final:
        ms = jnp.mean(x2 * x2, axis=-1, keepdims=True)
        x2 = x2 * lax.rsqrt(ms + EPS) * fg_ref[...]
    o_ref[0] = x2


def _out_ffn(x, mod, og, gg, orr, rg, osb, gng, rng, bd, wout, n2g, wg, wu, wd, fg, final):
    b, s, d = x.shape
    tm = min(TOKEN_TILE, s)
    tok = lambda width: pl.BlockSpec((1, tm, width), lambda bi, i: (bi, i, 0))
    const = lambda a: pl.BlockSpec(a.shape, lambda bi, i: tuple(0 for _ in a.shape),
                                   pipeline_mode=pl.Buffered(1))
    return pl.pallas_call(
        functools.partial(_ffn_kernel, final=final),
        grid=(b, s // tm),
        in_specs=[tok(d), pl.BlockSpec((1, 1, 6 * d), lambda bi, i: (bi, 0, 0)),
                  tok(GLA_V), tok(GLA_V), tok(RET_W), tok(RET_W), tok(SB_W),
                  const(gng), const(rng), const(bd), const(wout), const(n2g),
                  const(wg), const(wu), const(wd), const(fg)],
        out_specs=tok(d),
        out_shape=jax.ShapeDtypeStruct((b, s, d), F32),
        scratch_shapes=[pltpu.VMEM((tm, D_FF), BF16)],
        compiler_params=_params("arbitrary", "arbitrary"),
        name="out_projection_ffn",
    )(x, mod, og, gg, orr, rg, osb, gng, rng, bd, wout, n2g, wg, wu, wd, fg)


def _rope_tables(s):
    inv = ROPE_BASE ** (-np.arange(0, RET_DK, 2, dtype=np.float64) / RET_DK)
    ang = np.arange(s, dtype=np.float64)[:, None] * inv[None, :]
    cos = np.repeat(np.cos(ang), 2, axis=1)
    sin = np.repeat(np.sin(ang), 2, axis=1) * np.tile([-1.0, 1.0], RET_DK // 2)[None, :]
    reps = LANES // RET_DK
    return jnp.asarray(np.tile(cos, (1, reps)), F32), jnp.asarray(np.tile(sin, (1, reps)), F32)


def kernel(x, c, ada_w, ada_b, norm1_g, norm2_g, w_in, gla_wa2, gla_ba, gla_norm_g, ret_norm_g,
           w_out, ffn_wg, ffn_wu, ffn_wd, final_g):
    b, s, d = x.shape
    mods = _modulation(c, ada_w, ada_b)
    cos, sin = _rope_tables(s)
    gla_tables = _gla_tables()
    ret_tables = _ret_tables()
    hd = np.arange(GLA_V)
    bd = jnp.asarray((hd[:, None] // GLA_DV == hd[None, :] // GLA_DV) / float(GLA_DV), BF16)
    gr0 = C_RQ
    for l in range(DEPTH):
        w = w_in[l]
        w_perm = jnp.concatenate(
            [w[:, :gr0], w[:, gr0 + GLA_RANK:], w[:, gr0:gr0 + GLA_RANK],
             jnp.zeros((d, IN_COLS_PADDED - C_GR - GLA_RANK), w.dtype)], axis=1).astype(BF16)
        wa2 = jnp.concatenate([gla_wa2[l], jnp.zeros((LANES - GLA_RANK, GLA_QK), F32)], axis=0).astype(BF16)
        mod = mods[l].reshape(b, 1, 6 * d)
        (gq, gk, gv, gg, la, rq, rk, rv, rg, sq, skT, sv) = _input_projection(
            x, mod, norm1_g[l].reshape(1, d), w_perm, wa2, gla_ba[l].reshape(1, GLA_QK), cos, sin)
        o_gla = _gla(gq, gk, gv, la, gla_tables)
        o_ret = _retention(rq, rk, rv, ret_tables)
        o_sb = _stick_breaking(sq, skT, sv)
        x = _out_ffn(x, mod, o_gla, gg, o_ret, rg, o_sb,
                     jnp.tile(gla_norm_g[l], GLA_HEADS).reshape(1, GLA_V),
                     jnp.tile(ret_norm_g[l], RET_HEADS).reshape(1, RET_W), bd,
                     w_out[l].astype(BF16), norm2_g[l].reshape(1, d),
                     ffn_wg[l].astype(BF16), ffn_wu[l].astype(BF16), ffn_wd[l].astype(BF16),
                     final_g.reshape(1, d), l == DEPTH - 1)
    return x
```

```python
import functools

import jax
import jax.numpy as jnp
import numpy as np
from jax import lax
from jax.experimental import pallas as pl
from jax.experimental.pallas import tpu as pltpu

F32 = jnp.float32
BF16 = jnp.bfloat16

D_MODEL = 1024
DEPTH = 2
GLA_HEADS, GLA_DK, GLA_DV, GLA_RANK = 4, 32, 64, 16
GLA_GATE_NORM = 16.0
RET_HEADS, RET_DK, RET_DV = 4, 64, 64
ROPE_BASE = 10000.0
SB_HEADS, SB_DK, SB_DV = 8, 64, 64
D_FF = 2816
EPS = 1e-6

GLA_QK = GLA_HEADS * GLA_DK
GLA_V = GLA_HEADS * GLA_DV
RET_W = RET_HEADS * RET_DK
SB_W = SB_HEADS * SB_DK
C_GQ, C_GK, C_GV, C_GG = 0, 128, 256, 512
C_RQ, C_RK, C_RV, C_RG = 768, 1024, 1280, 1536
C_SQ, C_SK, C_SV = 1792, 2304, 2816
C_GR = 3328
IN_COLS_PADDED = 3456

LANES = 128
GLA_CHUNK = 64
GLA_LEVELS = 6
RET_CHUNK = 256
SB_TQ = 512
SB_TK = 256
SB_PAIRS = 2
SB_UNDERFLOW_LOG2 = 160.0
SB_EXP_CLAMP = 64.0
TOKEN_TILE = 512
FF_CHUNK = 256
VMEM_LIMIT = 56 * 1024 * 1024


def _dot(a, b):
    return jnp.dot(a, b, preferred_element_type=F32)


def _dot_nt(a, b):
    return lax.dot_general(a, b, (((1,), (1,)), ((), ())), preferred_element_type=F32)


def _dot_tn(a, b):
    return lax.dot_general(a, b, (((0,), (0,)), ((), ())), preferred_element_type=F32)


def _sigmoid(x):
    return 1.0 / (1.0 + jnp.exp(-x))


def _softplus(x):
    return jnp.maximum(x, 0.0) + jnp.log(1.0 + jnp.exp(-jnp.abs(x)))


def _log2e(width):
    return 1.0 / jnp.log(jnp.full((1, width), 2.0, F32))


def _params(*sem):
    return pltpu.CompilerParams(dimension_semantics=sem, vmem_limit_bytes=VMEM_LIMIT)


def _mod_kernel(c_ref, w_ref, b_ref, o_ref):
    c = c_ref[...]
    cs = c * _sigmoid(c)
    o_ref[0] = jnp.dot(cs, w_ref[0], preferred_element_type=F32,
                       precision=lax.Precision.HIGHEST) + b_ref[0]


def _modulation(c, ada_w, ada_b):
    depth, d, n = ada_w.shape
    b = c.shape[0]
    tn = 1536
    return pl.pallas_call(
        _mod_kernel,
        grid=(depth, n // tn),
        in_specs=[pl.BlockSpec((b, d), lambda l, j: (0, 0)),
                  pl.BlockSpec((1, d, tn), lambda l, j: (l, 0, j)),
                  pl.BlockSpec((1, 1, tn), lambda l, j: (l, 0, j))],
        out_specs=pl.BlockSpec((1, b, tn), lambda l, j: (l, 0, j)),
        out_shape=jax.ShapeDtypeStruct((depth, b, n), F32),
        compiler_params=_params("arbitrary", "arbitrary"),
        name="adaln_modulation",
    )(c, ada_w, ada_b.reshape(depth, 1, n))


def _rope(t, cos, sin):
    lane = lax.broadcasted_iota(jnp.int32, cos.shape, 1)
    even = (lane & 1) == 0
    outs = []
    for c in range(t.shape[1] // LANES):
        tc = t[:, c * LANES:(c + 1) * LANES]
        nxt = pltpu.roll(tc, LANES - 1, 1)
        prv = pltpu.roll(tc, 1, 1)
        outs.append(tc * cos + jnp.where(even, nxt, prv) * sin)
    return jnp.concatenate(outs, axis=1)


def _proj_mix_kernel(x_ref, mod_ref, g_ref, w_ref, wa2_ref, ba_ref, cos_ref, sin_ref,
                     mcat_ref, lmask_ref, kmask_ref, vmask_ref, gsmask_ref,
                     dall_ref, xi_ref, zeta_ref, gc_ref, hmask_ref, rsmask_ref,
                     og_ref, gg_ref, or_ref, rg_ref, sq_ref, skT_ref, sv_ref,
                     gst_ref, rst_ref):
    d = D_MODEL

    @pl.when(pl.program_id(1) == 0)
    def _():
        gst_ref[...] = jnp.zeros_like(gst_ref)
        rst_ref[...] = jnp.zeros_like(rst_ref)

    x = x_ref[0]
    mod = mod_ref[0]
    sh1, sc1 = mod[:, 0:d], mod[:, d:2 * d]
    ms = jnp.mean(x * x, axis=-1, keepdims=True)
    h = (x * lax.rsqrt(ms + EPS) * (g_ref[...] * (1.0 + sc1)) + sh1).astype(BF16)

    def proj(a, width):
        return _dot(h, w_ref[:, a:a + width])

    gqk = proj(C_GQ, 2 * GLA_QK)
    gq = gqk[:, :GLA_QK] * (GLA_DK ** -0.5)
    gk = gqk[:, GLA_QK:]
    gv = proj(C_GV, GLA_V)
    gg_ref[0] = proj(C_GG, GLA_V)
    sv_head = proj(C_SV, SB_W - LANES)
    sv_tail_gr = proj(C_GR - LANES, 2 * LANES)
    u = _dot(sv_tail_gr[:, LANES:].astype(BF16), wa2_ref[...]) + ba_ref[...]
    la = -_softplus(-u) * (1.0 / GLA_GATE_NORM)

    cos, sin = cos_ref[...], sin_ref[...]
    rq = _rope(proj(C_RQ, RET_W), cos, sin)
    rk = _rope(proj(C_RK, RET_W), cos, sin) * (RET_DK ** -0.5)
    rv = proj(C_RV, RET_W)
    rg_ref[0] = proj(C_RG, RET_W)

    sq_ref[0] = (proj(C_SQ, SB_W) * (_log2e(SB_W) * SB_DK ** -0.5)).astype(BF16)
    skT_ref[0] = proj(C_SK, SB_W).T.astype(BF16)
    sv_ref[0] = jnp.concatenate([sv_head, sv_tail_gr[:, :LANES]], axis=1).astype(BF16)

    og_ref[0] = _gla_block(gq, gk, gv, la, mcat_ref, lmask_ref, kmask_ref, vmask_ref, gsmask_ref, gst_ref)
    rc = RET_CHUNK
    for ci in range(x.shape[0] // rc):
        rows = slice(ci * rc, (ci + 1) * rc)
        or_ref[0, rows, :] = _ret_chunk(rq[rows], rk[rows], rv[rows], dall_ref, xi_ref, zeta_ref, gc_ref,
                                        hmask_ref, rsmask_ref, rst_ref)


def _project_and_mix(x, mod, g, w, wa2, ba, cos, sin, gla_tables, ret_tables):
    b, s, d = x.shape
    tm = TOKEN_TILE
    assert s % tm == 0 and tm % GLA_CHUNK == 0 and tm % RET_CHUNK == 0
    tok = lambda width: pl.BlockSpec((1, tm, width), lambda bi, i: (bi, i, 0))
    tokT = lambda width: pl.BlockSpec((1, width, tm), lambda bi, i: (bi, 0, i))
    const = lambda a: pl.BlockSpec(a.shape, lambda bi, i: tuple(0 for _ in a.shape))
    f32 = lambda width: jax.ShapeDtypeStruct((b, s, width), F32)
    consts = (g, w, wa2, ba)
    return pl.pallas_call(
        _proj_mix_kernel,
        grid=(b, s // tm),
        in_specs=[tok(d), pl.BlockSpec((1, 1, 6 * d), lambda bi, i: (bi, 0, 0))]
                 + [const(a) for a in consts]
                 + [pl.BlockSpec((tm, LANES), lambda bi, i: (i, 0))] * 2
                 + [const(a) for a in gla_tables + ret_tables],
        out_specs=[tok(GLA_V), tok(GLA_V), tok(RET_W), tok(RET_W), tok(SB_W), tokT(SB_W), tok(SB_W)],
        out_shape=[f32(GLA_V), f32(GLA_V), f32(RET_W), f32(RET_W),
                   jax.ShapeDtypeStruct((b, s, SB_W), BF16),
                   jax.ShapeDtypeStruct((b, SB_W, s), BF16),
                   jax.ShapeDtypeStruct((b, s, SB_W), BF16)],
        scratch_shapes=[pltpu.VMEM((GLA_V, GLA_QK), F32), pltpu.VMEM((RET_W, RET_W), F32)],
        compiler_params=_params("arbitrary", "arbitrary"),
        name="project_gla_retention",
    )(x, mod, *consts, cos, sin, *gla_tables, *ret_tables)


def _gla_tables():
    c, nl = GLA_CHUNK, GLA_LEVELS
    t = np.arange(c)
    rows = []
    for l in range(nl):
        w = 1 << l
        start = (t // w) * w
        rows.append((t[None, :] >= start[:, None]) & (t[None, :] <= t[:, None]))
    for l in range(nl):
        w = 1 << l
        end = (t // w + 1) * w - 1
        rows.append((t[None, :] > t[:, None]) & (t[None, :] <= end[:, None]))
    rows.append(t[None, :] <= t[:, None])
    mcat = np.concatenate(rows, 0).astype(np.float32)
    mcat3 = np.concatenate([mcat] * 3, 1)
    masks = []
    for l in range(nl):
        w = 1 << l
        same = (t[:, None] // (2 * w)) == (t[None, :] // (2 * w))
        upper = ((t // w) % 2 == 1)[:, None]
        lower = ((t // w) % 2 == 0)[None, :]
        masks.append(np.tile(same & upper & lower, (1, GLA_HEADS)))
    masks.append(np.tile(np.eye(c, dtype=bool), (1, GLA_HEADS)))
    lmask = np.stack(masks).astype(np.float32)
    r = np.arange(GLA_HEADS * c)
    kmask = (r[:, None] // c == np.arange(GLA_QK)[None, :] // GLA_DK).astype(np.float32)
    vmask = (r[:, None] // c == np.arange(GLA_V)[None, :] // GLA_DV).astype(np.float32)
    smask = (np.arange(GLA_V)[:, None] // GLA_DV == np.arange(GLA_QK)[None, :] // GLA_DK).astype(np.float32)
    return (jnp.asarray(mcat3, BF16), jnp.asarray(lmask), jnp.asarray(kmask),
            jnp.asarray(vmask), jnp.asarray(smask))


def _split3(x):
    hi = x.astype(BF16)
    r1 = x - hi.astype(F32)
    mid = r1.astype(BF16)
    lo = (r1 - mid.astype(F32)).astype(BF16)
    return hi, mid, lo


def _gla_block(q_all, k_all, v_all, la_all, mcat_ref, lmask_ref, kmask_ref, vmask_ref, smask_ref, st_ref):
    c, nl = GLA_CHUNK, GLA_LEVELS
    nchunk = q_all.shape[0] // c
    kmask, vmask, smask = kmask_ref[...], vmask_ref[...], smask_ref[...]
    la3 = jnp.concatenate([jnp.concatenate(_split3(la_all[ci * c:(ci + 1) * c]), axis=0)
                           for ci in range(nchunk)], axis=1)
    cs_all = _dot(mcat_ref[...], la3)
    e_all = jnp.exp(cs_all)
    outs = []
    for ci in range(nchunk):
        sl = slice(ci * c, (ci + 1) * c)
        q, k, v = q_all[sl], k_all[sl], v_all[sl]
        cs = cs_all[:, ci * GLA_QK:(ci + 1) * GLA_QK]
        e = e_all[:, ci * GLA_QK:(ci + 1) * GLA_QK]
        scores = jnp.zeros((c, GLA_HEADS * c), F32)
        for l in range(nl + 1):
            if l < nl:
                qt = q * e[l * c:(l + 1) * c]
                kt = k * e[(nl + l) * c:(nl + l + 1) * c]
            else:
                qt, kt = q, k
            krows = (jnp.concatenate([kt] * GLA_HEADS, axis=0) * kmask).astype(BF16)
            scores = scores + _dot_nt(qt.astype(BF16), krows) * lmask_ref[l]
        vbd = (jnp.concatenate([v] * GLA_HEADS, axis=0) * vmask).astype(BF16)
        o = _dot(scores.astype(BF16), vbd)
        bcum = cs[2 * nl * c:(2 * nl + 1) * c]
        st = st_ref[...]
        o = o + _dot_nt((q * e[2 * nl * c:(2 * nl + 1) * c]).astype(BF16), st.astype(BF16))
        blast = bcum[c - 1:c, :]
        kd = k * jnp.exp(blast - bcum)
        st_ref[...] = st * jnp.exp(blast) + _dot_tn(v.astype(BF16), kd.astype(BF16)) * smask
        outs.append(o)
    return jnp.concatenate(outs, axis=0)


def _ret_tables():
    c = RET_CHUNK
    gam = 1.0 - 2.0 ** (-5.0 - np.arange(RET_HEADS, dtype=np.float64))
    t = np.arange(c)
    rel = t[:, None] - t[None, :]
    dmat = np.where(rel[None] >= 0, gam[:, None, None] ** np.maximum(rel, 0)[None], 0.0)
    dall = np.concatenate(list(dmat), axis=1)
    xi = np.repeat((gam[None, :] ** (t[:, None] + 1.0)), RET_DK, axis=1)
    zeta = np.repeat((gam[None, :] ** (c - 1.0 - t[:, None])), RET_DK, axis=1)
    gc = np.repeat(gam ** c, RET_DK)[:, None] * np.ones((1, RET_W))
    r = np.arange(RET_HEADS * c)
    hmask = (r[:, None] // c == np.arange(RET_W)[None, :] // RET_DK)
    smask = (np.arange(RET_W)[:, None] // RET_DK == np.arange(RET_W)[None, :] // RET_DV)
    return tuple(jnp.asarray(a, F32) for a in (dall, xi, zeta, gc, hmask, smask))


def _ret_chunk(q, k, v, dall_ref, xi_ref, zeta_ref, gc_ref, hmask_ref, smask_ref, r_ref):
    hmask = hmask_ref[...]
    krows = (jnp.concatenate([k] * RET_HEADS, axis=0) * hmask).astype(BF16)
    scores = _dot_nt(q.astype(BF16), krows) * dall_ref[...]
    vbd = (jnp.concatenate([v] * RET_HEADS, axis=0) * hmask).astype(BF16)
    o = _dot(scores.astype(BF16), vbd)
    r = r_ref[...]
    o = o + _dot((q * xi_ref[...]).astype(BF16), r.astype(BF16))
    upd = _dot_tn((k * zeta_ref[...]).astype(BF16), v.astype(BF16))
    r_ref[...] = r * gc_ref[...] + upd * smask_ref[...]
    return o


def _sb_kernel(q_ref, kT_ref, v_ref, w_ref, o_ref, acc_ref, c_ref):
    tq, tk = SB_TQ, SB_TK
    i = pl.program_id(2)
    lane = lax.broadcasted_iota(jnp.int32, (tq, LANES), 1)
    qs = []
    for pr in range(SB_PAIRS):
        q2 = q_ref[0, :, pr * LANES:(pr + 1) * LANES]
        zero = jnp.zeros_like(q2)
        qs += [jnp.where(lane < SB_DK, q2, zero), jnp.where(lane >= SB_DK, q2, zero)]
    acc_ref[...] = jnp.zeros_like(acc_ref)
    c_ref[...] = jnp.zeros_like(c_ref)
    w = w_ref[...]
    log2e = _log2e(tk)

    def tile(j, r0, r1, mask_rows):
        start = pl.multiple_of(j * tk, tk)
        sections = [(r0, r0 + mask_rows, True), (r0 + mask_rows, r1, False)]
        for pr in range(SB_PAIRS):
            kT2 = kT_ref[0, pr * LANES:(pr + 1) * LANES, pl.ds(start, tk)]
            v2 = v_ref[0, pl.ds(start, tk), pr * LANES:(pr + 1) * LANES]
            for hd in range(2 * pr, 2 * pr + 2):
                z = _dot(qs[hd][r0:r1], kT2)
                for a, b, masked in sections:
                    if a == b:
                        continue
                    zs = z[a - r0:b - r0]
                    c = c_ref[hd, a:b, :]
                    sp = jnp.maximum(jnp.log(1.0 + jnp.exp2(jnp.minimum(zs, SB_EXP_CLAMP))) * log2e, zs)
                    if masked:
                        key_s = lax.broadcasted_iota(jnp.int32, zs.shape, 1) + start
                        query_t = lax.broadcasted_iota(jnp.int32, zs.shape, 0) + (i * tq + a)
                        causal = key_s < query_t
                        sp = jnp.where(causal, sp, 0.0)
                    r = _dot(sp.astype(BF16), w)
                    p = jnp.exp2(zs - (r + jnp.concatenate([c] * (tk // LANES), axis=1)))
                    if masked:
                        p = jnp.where(causal, p, 0.0)
                    c_ref[hd, a:b, :] = c + jnp.broadcast_to(r[:, 0:1], c.shape)
                    acc_ref[hd, a:b, :] += _dot(p.astype(BF16), v2)

    nk = tq // tk
    half = tq // 2

    def diagonal():
        for d in reversed(range(nk)):
            tile(i * nk + d, d * tk, tq, tk)

    @pl.when(i == 0)
    def _():
        diagonal()

    @pl.when(i > 0)
    def _():
        diagonal()
        tile(i * nk - 1, 0, half, 0)

    def mins():
        c = c_ref[...]
        return jnp.min(c[:, :half]), jnp.min(c[:, half:])

    def cond(st):
        return jnp.logical_and(st[0] < i * nk, jnp.minimum(st[1], st[2]) < SB_UNDERFLOW_LOG2)

    def body(st):
        j = i * nk - 1 - st[0]
        lower_done = st[2] >= SB_UNDERFLOW_LOG2

        def first():
            @pl.when(jnp.logical_not(lower_done))
            def _():
                tile(j, half, tq, 0)

        def later():
            lax.cond(lower_done, lambda: tile(j, 0, half, 0), lambda: tile(j, 0, tq, 0))

        lax.cond(st[0] == 0, first, later)
        return (st[0] + 1,) + mins()

    lax.while_loop(cond, body, (jnp.int32(0),) + mins())
    for pr in range(SB_PAIRS):
        o_ref[0, :, pr * LANES:(pr + 1) * LANES] = jnp.where(
            lane < SB_DK, acc_ref[2 * pr], acc_ref[2 * pr + 1]).astype(BF16)


def _stick_breaking(q, kT, v):
    b, s, w = q.shape
    tq = SB_TQ
    wd = SB_PAIRS * LANES
    r = np.arange(SB_TK)
    wmat = jnp.asarray(r[:, None] >= r[None, :], BF16)
    return pl.pallas_call(
        _sb_kernel,
        grid=(b, w // wd, s // tq),
        in_specs=[pl.BlockSpec((1, tq, wd), lambda bi, p, i: (bi, i, p)),
                  pl.BlockSpec((1, wd, s), lambda bi, p, i: (bi, p, 0)),
                  pl.BlockSpec((1, s, wd), lambda bi, p, i: (bi, 0, p)),
                  pl.BlockSpec(wmat.shape, lambda bi, p, i: (0, 0))],
        out_specs=pl.BlockSpec((1, tq, wd), lambda bi, p, i: (bi, i, p)),
        out_shape=jax.ShapeDtypeStruct((b, s, w), BF16),
        scratch_shapes=[pltpu.VMEM((2 * SB_PAIRS, tq, LANES), F32), pltpu.VMEM((2 * SB_PAIRS, tq, LANES), F32)],
        compiler_params=_params("arbitrary", "arbitrary", "arbitrary"),
        name="stick_breaking_mixer",
    )(q, kT, v, wmat)


def _head_norm_gate(o, g, gate, bd):
    sq = o * o
    hi = sq.astype(BF16)
    lo = (sq - hi.astype(F32)).astype(BF16)
    ms = _dot(hi, bd) + _dot(lo, bd)
    return o * lax.rsqrt(ms + EPS) * g * (gate * _sigmoid(gate))


def _ffn_kernel(x_ref, mod_ref, og_ref, gg_ref, or_ref, rg_ref, osb_ref, gng_ref, rng_ref, bd_ref,
                wout_ref, n2g_ref, wg_ref, wu_ref, wd_ref, fg_ref, o_ref, p_ref, *, final):
    d = D_MODEL
    x = x_ref[0]
    mod = mod_ref[0]
    g1, sh2, sc2, g2 = mod[:, 2 * d:3 * d], mod[:, 3 * d:4 * d], mod[:, 4 * d:5 * d], mod[:, 5 * d:6 * d]
    bd = bd_ref[...]
    o_gla = _head_norm_gate(og_ref[0], gng_ref[...], gg_ref[0], bd)
    o_ret = _head_norm_gate(or_ref[0], rng_ref[...], rg_ref[0], bd)
    o = jnp.concatenate([o_gla.astype(BF16), o_ret.astype(BF16), osb_ref[0]], axis=1)
    x1 = x + g1 * _dot(o, wout_ref[...])
    ms = jnp.mean(x1 * x1, axis=-1, keepdims=True)
    h = (x1 * lax.rsqrt(ms + EPS) * (n2g_ref[...] * (1.0 + sc2)) + sh2).astype(BF16)
    for c0 in range(0, D_FF, FF_CHUNK):
        a = _dot(h, wg_ref[:, c0:c0 + FF_CHUNK])
        u = _dot(h, wu_ref[:, c0:c0 + FF_CHUNK])
        p_ref[:, c0:c0 + FF_CHUNK] = (a * _sigmoid(a) * u).astype(BF16)
    x2 = x1 + g2 * _dot(p_ref[...], wd_ref[...])
    if final:
        ms = jnp.mean(x2 * x2, axis=-1, keepdims=True)
        x2 = x2 * lax.rsqrt(ms + EPS) * fg_ref[...]
    o_ref[0] = x2


def _out_ffn(x, mod, og, gg, orr, rg, osb, gng, rng, bd, wout, n2g, wg, wu, wd, fg, final):
    b, s, d = x.shape
    tm = min(TOKEN_TILE, s)
    tok = lambda width: pl.BlockSpec((1, tm, width), lambda bi, i: (bi, i, 0))
    const = lambda a: pl.BlockSpec(a.shape, lambda bi, i: tuple(0 for _ in a.shape),
                                   pipeline_mode=pl.Buffered(1))
    return pl.pallas_call(
        functools.partial(_ffn_kernel, final=final),
        grid=(b, s // tm),
        in_specs=[tok(d), pl.BlockSpec((1, 1, 6 * d), lambda bi, i: (bi, 0, 0)),
                  tok(GLA_V), tok(GLA_V), tok(RET_W), tok(RET_W), tok(SB_W),
                  const(gng), const(rng), const(bd), const(wout), const(n2g),
                  const(wg), const(wu), const(wd), const(fg)],
        out_specs=tok(d),
        out_shape=jax.ShapeDtypeStruct((b, s, d), F32),
        scratch_shapes=[pltpu.VMEM((tm, D_FF), BF16)],
        compiler_params=_params("arbitrary", "arbitrary"),
        name="out_projection_ffn",
    )(x, mod, og, gg, orr, rg, osb, gng, rng, bd, wout, n2g, wg, wu, wd, fg)


def _rope_tables(s):
    inv = ROPE_BASE ** (-np.arange(0, RET_DK, 2, dtype=np.float64) / RET_DK)
    ang = np.arange(s, dtype=np.float64)[:, None] * inv[None, :]
    cos = np.repeat(np.cos(ang), 2, axis=1)
    sin = np.repeat(np.sin(ang), 2, axis=1) * np.tile([-1.0, 1.0], RET_DK // 2)[None, :]
    reps = LANES // RET_DK
    return jnp.asarray(np.tile(cos, (1, reps)), F32), jnp.asarray(np.tile(sin, (1, reps)), F32)


def kernel(x, c, ada_w, ada_b, norm1_g, norm2_g, w_in, gla_wa2, gla_ba, gla_norm_g, ret_norm_g,
           w_out, ffn_wg, ffn_wu, ffn_wd, final_g):
    b, s, d = x.shape
    mods = _modulation(c, ada_w, ada_b)
    cos, sin = _rope_tables(s)
    gla_tables = _gla_tables()
    ret_tables = _ret_tables()
    hd = np.arange(GLA_V)
    bd = jnp.asarray((hd[:, None] // GLA_DV == hd[None, :] // GLA_DV) / float(GLA_DV), BF16)
    gr0 = 2 * GLA_QK + 2 * GLA_V
    for l in range(DEPTH):
        w = w_in[l]
        w_perm = jnp.concatenate(
            [w[:, :gr0], w[:, gr0 + GLA_RANK:], w[:, gr0:gr0 + GLA_RANK],
             jnp.zeros((d, IN_COLS_PADDED - C_GR - GLA_RANK), w.dtype)], axis=1).astype(BF16)
        wa2 = jnp.concatenate([gla_wa2[l], jnp.zeros((LANES - GLA_RANK, GLA_QK), F32)], axis=0).astype(BF16)
        mod = mods[l].reshape(b, 1, 6 * d)
        o_gla, gg, o_ret, rg, sq, skT, sv = _project_and_mix(
            x, mod, norm1_g[l].reshape(1, d), w_perm, wa2, gla_ba[l].reshape(1, GLA_QK), cos, sin,
            gla_tables, ret_tables)
        o_sb = _stick_breaking(sq, skT, sv)
        x = _out_ffn(x, mod, o_gla, gg, o_ret, rg, o_sb,
                     jnp.tile(gla_norm_g[l], GLA_HEADS).reshape(1, GLA_V),
                     jnp.tile(ret_norm_g[l], RET_HEADS).reshape(1, RET_W), bd,
                     w_out[l].astype(BF16), norm2_g[l].reshape(1, d),
                     ffn_wg[l].astype(BF16), ffn_wu[l].astype(BF16), ffn_wd[l].astype(BF16),
                     final_g.reshape(1, d), l == DEPTH - 1)
    return x
```

```python
import functools

import jax
import jax.numpy as jnp
import numpy as np
from jax import lax
from jax.experimental import pallas as pl
from jax.experimental.pallas import tpu as pltpu

F32 = jnp.float32
BF16 = jnp.bfloat16

D_MODEL = 1024
DEPTH = 2
GLA_HEADS, GLA_DK, GLA_DV, GLA_RANK = 4, 32, 64, 16
GLA_GATE_NORM = 16.0
RET_HEADS, RET_DK, RET_DV = 4, 64, 64
ROPE_BASE = 10000.0
SB_HEADS, SB_DK, SB_DV = 8, 64, 64
D_FF = 2816
EPS = 1e-6

GLA_QK = GLA_HEADS * GLA_DK
GLA_V = GLA_HEADS * GLA_DV
RET_W = RET_HEADS * RET_DK
SB_W = SB_HEADS * SB_DK
C_GQ, C_GK, C_GV, C_GG = 0, 128, 256, 512
C_RQ, C_RK, C_RV, C_RG = 768, 1024, 1280, 1536
C_SQ, C_SK, C_SV = 1792, 2304, 2816
C_GR = 3328
IN_COLS_PADDED = 3456

LANES = 128
GLA_CHUNK = 64
GLA_LEVELS = 6
RET_CHUNK = 256
SB_TQ = 512
SB_TK = 256
SB_UNDERFLOW_LOG2 = 160.0
SB_EXP_CLAMP = 64.0
TOKEN_TILE = 512
FF_CHUNK = 256
VMEM_LIMIT = 56 * 1024 * 1024


def _dot(a, b):
    return jnp.dot(a, b, preferred_element_type=F32)


def _dot_nt(a, b):
    return lax.dot_general(a, b, (((1,), (1,)), ((), ())), preferred_element_type=F32)


def _dot_tn(a, b):
    return lax.dot_general(a, b, (((0,), (0,)), ((), ())), preferred_element_type=F32)


def _sigmoid(x):
    return 1.0 / (1.0 + jnp.exp(-x))


def _softplus(x):
    return jnp.maximum(x, 0.0) + jnp.log(1.0 + jnp.exp(-jnp.abs(x)))


def _log2e(width):
    return 1.0 / jnp.log(jnp.full((1, width), 2.0, F32))


def _params(*sem):
    return pltpu.CompilerParams(dimension_semantics=sem, vmem_limit_bytes=VMEM_LIMIT)


def _mod_kernel(c_ref, w_ref, b_ref, o_ref):
    c = c_ref[...]
    cs = c * _sigmoid(c)
    o_ref[0] = jnp.dot(cs, w_ref[0], preferred_element_type=F32,
                       precision=lax.Precision.HIGHEST) + b_ref[0]


def _modulation(c, ada_w, ada_b):
    depth, d, n = ada_w.shape
    b = c.shape[0]
    tn = 1536
    return pl.pallas_call(
        _mod_kernel,
        grid=(depth, n // tn),
        in_specs=[pl.BlockSpec((b, d), lambda l, j: (0, 0)),
                  pl.BlockSpec((1, d, tn), lambda l, j: (l, 0, j)),
                  pl.BlockSpec((1, 1, tn), lambda l, j: (l, 0, j))],
        out_specs=pl.BlockSpec((1, b, tn), lambda l, j: (l, 0, j)),
        out_shape=jax.ShapeDtypeStruct((depth, b, n), F32),
        compiler_params=_params("arbitrary", "arbitrary"),
        name="adaln_modulation",
    )(c, ada_w, ada_b.reshape(depth, 1, n))


def _rope(t, cos, sin):
    lane = lax.broadcasted_iota(jnp.int32, cos.shape, 1)
    even = (lane & 1) == 0
    outs = []
    for c in range(t.shape[1] // LANES):
        tc = t[:, c * LANES:(c + 1) * LANES]
        nxt = pltpu.roll(tc, LANES - 1, 1)
        prv = pltpu.roll(tc, 1, 1)
        outs.append(tc * cos + jnp.where(even, nxt, prv) * sin)
    return jnp.concatenate(outs, axis=1)


def _proj_mix_kernel(x_ref, mod_ref, g_ref, w_ref, wa2_ref, ba_ref, cos_ref, sin_ref,
                     mcat_ref, lmask_ref, kmask_ref, vmask_ref, gsmask_ref,
                     dall_ref, xi_ref, zeta_ref, gc_ref, hmask_ref, rsmask_ref, tri_ref,
                     og_ref, gg_ref, or_ref, rg_ref, osb_ref,
                     gst_ref, rst_ref, skT_ref, sv_ref, acc_ref, c_ref):
    d = D_MODEL
    i = pl.program_id(1)

    @pl.when(pl.program_id(1) == 0)
    def _():
        gst_ref[...] = jnp.zeros_like(gst_ref)
        rst_ref[...] = jnp.zeros_like(rst_ref)

    x = x_ref[0]
    mod = mod_ref[0]
    sh1, sc1 = mod[:, 0:d], mod[:, d:2 * d]
    ms = jnp.mean(x * x, axis=-1, keepdims=True)
    h = (x * lax.rsqrt(ms + EPS) * (g_ref[...] * (1.0 + sc1)) + sh1).astype(BF16)

    def proj(a, width):
        return _dot(h, w_ref[:, a:a + width])

    gqk = proj(C_GQ, 2 * GLA_QK)
    gq = gqk[:, :GLA_QK] * (GLA_DK ** -0.5)
    gk = gqk[:, GLA_QK:]
    gv = proj(C_GV, GLA_V)
    gg_ref[0] = proj(C_GG, GLA_V)
    sv_head = proj(C_SV, SB_W - LANES)
    sv_tail_gr = proj(C_GR - LANES, 2 * LANES)
    u = _dot(sv_tail_gr[:, LANES:].astype(BF16), wa2_ref[...]) + ba_ref[...]
    la = -_softplus(-u) * (1.0 / GLA_GATE_NORM)

    cos, sin = cos_ref[...], sin_ref[...]
    rq = _rope(proj(C_RQ, RET_W), cos, sin)
    rk = _rope(proj(C_RK, RET_W), cos, sin) * (RET_DK ** -0.5)
    rv = proj(C_RV, RET_W)
    rg_ref[0] = proj(C_RG, RET_W)

    sq = (proj(C_SQ, SB_W) * (_log2e(SB_W) * SB_DK ** -0.5)).astype(BF16)
    skT = proj(C_SK, SB_W).T.astype(BF16)
    sv = jnp.concatenate([sv_head, sv_tail_gr[:, :LANES]], axis=1).astype(BF16)
    tokens = pl.ds(pl.multiple_of(i * SB_TQ, SB_TQ), SB_TQ)
    skT_ref[:, tokens] = skT
    sv_ref[tokens, :] = sv
    osb_ref[0] = _sb_block(i, sq, skT, sv, skT_ref, sv_ref, tri_ref, acc_ref, c_ref)

    og_ref[0] = _gla_block(gq, gk, gv, la, mcat_ref, lmask_ref, kmask_ref, vmask_ref, gsmask_ref, gst_ref)
    rc = RET_CHUNK
    for ci in range(x.shape[0] // rc):
        rows = slice(ci * rc, (ci + 1) * rc)
        or_ref[0, rows, :] = _ret_chunk(rq[rows], rk[rows], rv[rows], dall_ref, xi_ref, zeta_ref, gc_ref,
                                        hmask_ref, rsmask_ref, rst_ref)


def _project_and_mix(x, mod, g, w, wa2, ba, cos, sin, gla_tables, ret_tables):
    b, s, d = x.shape
    tm = TOKEN_TILE
    assert tm == SB_TQ and s % tm == 0 and tm % GLA_CHUNK == 0 and tm % RET_CHUNK == 0
    tok = lambda width: pl.BlockSpec((1, tm, width), lambda bi, i: (bi, i, 0))
    const = lambda a: pl.BlockSpec(a.shape, lambda bi, i: tuple(0 for _ in a.shape))
    f32 = lambda width: jax.ShapeDtypeStruct((b, s, width), F32)
    r = np.arange(SB_TK)
    tri = jnp.asarray(r[:, None] >= r[None, :], BF16)
    consts = (g, w, wa2, ba)
    return pl.pallas_call(
        _proj_mix_kernel,
        grid=(b, s // tm),
        in_specs=[tok(d), pl.BlockSpec((1, 1, 6 * d), lambda bi, i: (bi, 0, 0))]
                 + [const(a) for a in consts]
                 + [pl.BlockSpec((tm, LANES), lambda bi, i: (i, 0))] * 2
                 + [const(a) for a in gla_tables + ret_tables + (tri,)],
        out_specs=[tok(GLA_V), tok(GLA_V), tok(RET_W), tok(RET_W), tok(SB_W)],
        out_shape=[f32(GLA_V), f32(GLA_V), f32(RET_W), f32(RET_W),
                   jax.ShapeDtypeStruct((b, s, SB_W), BF16)],
        scratch_shapes=[pltpu.VMEM((GLA_V, GLA_QK), F32), pltpu.VMEM((RET_W, RET_W), F32),
                        pltpu.VMEM((SB_W, s), BF16), pltpu.VMEM((s, SB_W), BF16),
                        pltpu.VMEM((SB_HEADS, tm, LANES), F32), pltpu.VMEM((SB_HEADS, tm, LANES), F32)],
        compiler_params=_params("arbitrary", "arbitrary"),
        name="project_and_mix",
    )(x, mod, *consts, cos, sin, *gla_tables, *ret_tables, tri)


def _gla_tables():
    c, nl = GLA_CHUNK, GLA_LEVELS
    t = np.arange(c)
    rows = []
    for l in range(nl):
        w = 1 << l
        start = (t // w) * w
        rows.append((t[None, :] >= start[:, None]) & (t[None, :] <= t[:, None]))
    for l in range(nl):
        w = 1 << l
        end = (t // w + 1) * w - 1
        rows.append((t[None, :] > t[:, None]) & (t[None, :] <= end[:, None]))
    rows.append(t[None, :] <= t[:, None])
    mcat = np.concatenate(rows, 0).astype(np.float32)
    mcat3 = np.concatenate([mcat] * 3, 1)
    masks = []
    for l in range(nl):
        w = 1 << l
        same = (t[:, None] // (2 * w)) == (t[None, :] // (2 * w))
        upper = ((t // w) % 2 == 1)[:, None]
        lower = ((t // w) % 2 == 0)[None, :]
        masks.append(np.tile(same & upper & lower, (1, GLA_HEADS)))
    masks.append(np.tile(np.eye(c, dtype=bool), (1, GLA_HEADS)))
    lmask = np.stack(masks).astype(np.float32)
    r = np.arange(GLA_HEADS * c)
    kmask = (r[:, None] // c == np.arange(GLA_QK)[None, :] // GLA_DK).astype(np.float32)
    vmask = (r[:, None] // c == np.arange(GLA_V)[None, :] // GLA_DV).astype(np.float32)
    smask = (np.arange(GLA_V)[:, None] // GLA_DV == np.arange(GLA_QK)[None, :] // GLA_DK).astype(np.float32)
    return (jnp.asarray(mcat3, BF16), jnp.asarray(lmask), jnp.asarray(kmask),
            jnp.asarray(vmask), jnp.asarray(smask))


def _split3(x):
    hi = x.astype(BF16)
    r1 = x - hi.astype(F32)
    mid = r1.astype(BF16)
    lo = (r1 - mid.astype(F32)).astype(BF16)
    return hi, mid, lo


def _gla_block(q_all, k_all, v_all, la_all, mcat_ref, lmask_ref, kmask_ref, vmask_ref, smask_ref, st_ref):
    c, nl = GLA_CHUNK, GLA_LEVELS
    nchunk = q_all.shape[0] // c
    kmask, vmask, smask = kmask_ref[...], vmask_ref[...], smask_ref[...]
    la3 = jnp.concatenate([jnp.concatenate(_split3(la_all[ci * c:(ci + 1) * c]), axis=0)
                           for ci in range(nchunk)], axis=1)
    cs_all = _dot(mcat_ref[...], la3)
    e_all = jnp.exp(cs_all)
    outs = []
    for ci in range(nchunk):
        sl = slice(ci * c, (ci + 1) * c)
        q, k, v = q_all[sl], k_all[sl], v_all[sl]
        cs = cs_all[:, ci * GLA_QK:(ci + 1) * GLA_QK]
        e = e_all[:, ci * GLA_QK:(ci + 1) * GLA_QK]
        scores = jnp.zeros((c, GLA_HEADS * c), F32)
        for l in range(nl + 1):
            if l < nl:
                qt = q * e[l * c:(l + 1) * c]
                kt = k * e[(nl + l) * c:(nl + l + 1) * c]
            else:
                qt, kt = q, k
            krows = (jnp.concatenate([kt] * GLA_HEADS, axis=0) * kmask).astype(BF16)
            scores = scores + _dot_nt(qt.astype(BF16), krows) * lmask_ref[l]
        vbd = (jnp.concatenate([v] * GLA_HEADS, axis=0) * vmask).astype(BF16)
        o = _dot(scores.astype(BF16), vbd)
        bcum = cs[2 * nl * c:(2 * nl + 1) * c]
        st = st_ref[...]
        o = o + _dot_nt((q * e[2 * nl * c:(2 * nl + 1) * c]).astype(BF16), st.astype(BF16))
        blast = bcum[c - 1:c, :]
        kd = k * jnp.exp(blast - bcum)
        st_ref[...] = st * jnp.exp(blast) + _dot_tn(v.astype(BF16), kd.astype(BF16)) * smask
        outs.append(o)
    return jnp.concatenate(outs, axis=0)


def _ret_tables():
    c = RET_CHUNK
    gam = 1.0 - 2.0 ** (-5.0 - np.arange(RET_HEADS, dtype=np.float64))
    t = np.arange(c)
    rel = t[:, None] - t[None, :]
    dmat = np.where(rel[None] >= 0, gam[:, None, None] ** np.maximum(rel, 0)[None], 0.0)
    dall = np.concatenate(list(dmat), axis=1)
    xi = np.repeat((gam[None, :] ** (t[:, None] + 1.0)), RET_DK, axis=1)
    zeta = np.repeat((gam[None, :] ** (c - 1.0 - t[:, None])), RET_DK, axis=1)
    gc = np.repeat(gam ** c, RET_DK)[:, None] * np.ones((1, RET_W))
    r = np.arange(RET_HEADS * c)
    hmask = (r[:, None] // c == np.arange(RET_W)[None, :] // RET_DK)
    smask = (np.arange(RET_W)[:, None] // RET_DK == np.arange(RET_W)[None, :] // RET_DV)
    return tuple(jnp.asarray(a, F32) for a in (dall, xi, zeta, gc, hmask, smask))


def _ret_chunk(q, k, v, dall_ref, xi_ref, zeta_ref, gc_ref, hmask_ref, smask_ref, r_ref):
    hmask = hmask_ref[...]
    krows = (jnp.concatenate([k] * RET_HEADS, axis=0) * hmask).astype(BF16)
    scores = _dot_nt(q.astype(BF16), krows) * dall_ref[...]
    vbd = (jnp.concatenate([v] * RET_HEADS, axis=0) * hmask).astype(BF16)
    o = _dot(scores.astype(BF16), vbd)
    r = r_ref[...]
    o = o + _dot((q * xi_ref[...]).astype(BF16), r.astype(BF16))
    upd = _dot_tn((k * zeta_ref[...]).astype(BF16), v.astype(BF16))
    r_ref[...] = r * gc_ref[...] + upd * smask_ref[...]
    return o


def _sb_block(i, q, kT_new, v_new, kT_ref, v_ref, w_ref, acc_ref, c_ref):
    tq, tk = SB_TQ, SB_TK
    npairs = SB_HEADS // 2
    lane = lax.broadcasted_iota(jnp.int32, (tq, LANES), 1)
    qs = []
    for pr in range(npairs):
        q2 = q[:, pr * LANES:(pr + 1) * LANES]
        zero = jnp.zeros_like(q2)
        qs += [jnp.where(lane < SB_DK, q2, zero), jnp.where(lane >= SB_DK, q2, zero)]
    acc_ref[...] = jnp.zeros_like(acc_ref)
    c_ref[...] = jnp.zeros_like(c_ref)
    w = w_ref[...]
    log2e = _log2e(tk)

    def tile(j, r0, r1, mask_rows, local=None, valid=None):
        start = pl.multiple_of(j * tk, tk)
        sections = [(r0, r0 + mask_rows, True), (r0 + mask_rows, r1, False)]
        for pr in range(npairs):
            heads = slice(pr * LANES, (pr + 1) * LANES)
            if local is None:
                kT2 = kT_ref[heads, pl.ds(start, tk)]
                v2 = v_ref[pl.ds(start, tk), heads]
            else:
                kT2 = kT_new[heads, local * tk:(local + 1) * tk]
                v2 = v_new[local * tk:(local + 1) * tk, heads]
            for hd in range(2 * pr, 2 * pr + 2):
                z = _dot(qs[hd][r0:r1], kT2)
                for a, b, masked in sections:
                    if a == b:
                        continue
                    zs = z[a - r0:b - r0]
                    c = c_ref[hd, a:b, :]
                    sp = jnp.maximum(jnp.log(1.0 + jnp.exp2(jnp.minimum(zs, SB_EXP_CLAMP))) * log2e, zs)
                    if masked:
                        key_s = lax.broadcasted_iota(jnp.int32, zs.shape, 1) + start
                        query_t = lax.broadcasted_iota(jnp.int32, zs.shape, 0) + (i * tq + a)
                        causal = key_s < query_t
                        sp = jnp.where(causal, sp, 0.0)
                    if valid is not None:
                        sp = jnp.where(valid, sp, 0.0)
                    r = _dot(sp.astype(BF16), w)
                    p = jnp.exp2(zs - (r + jnp.concatenate([c] * (tk // LANES), axis=1)))
                    if masked:
                        p = jnp.where(causal, p, 0.0)
                    if valid is not None:
                        p = jnp.where(valid, p, 0.0)
                    c_ref[hd, a:b, :] = c + jnp.broadcast_to(r[:, 0:1], c.shape)
                    acc_ref[hd, a:b, :] += _dot(p.astype(BF16), v2)

    nk = tq // tk
    half = tq // 2
    for d in reversed(range(nk)):
        tile(i * nk + d, d * tk, tq, tk, local=d)
    tile(jnp.maximum(i * nk - 1, 0), 0, half, 0, valid=i > 0)

    def mins():
        c = c_ref[...]
        return jnp.min(c[:, :half]), jnp.min(c[:, half:])

    def cond(st):
        return jnp.logical_and(st[0] < i * nk, jnp.minimum(st[1], st[2]) < SB_UNDERFLOW_LOG2)

    def body(st):
        j = i * nk - 1 - st[0]
        lower_done = st[2] >= SB_UNDERFLOW_LOG2

        def first():
            @pl.when(jnp.logical_not(lower_done))
            def _():
                tile(j, half, tq, 0)

        def later():
            lax.cond(lower_done, lambda: tile(j, 0, half, 0), lambda: tile(j, 0, tq, 0))

        lax.cond(st[0] == 0, first, later)
        return (st[0] + 1,) + mins()

    lax.while_loop(cond, body, (jnp.int32(0),) + mins())
    return jnp.concatenate(
        [jnp.where(lane < SB_DK, acc_ref[2 * pr], acc_ref[2 * pr + 1]).astype(BF16) for pr in range(npairs)], axis=1)


def _head_norm_gate(o, g, gate, bd):
    sq = o * o
    hi = sq.astype(BF16)
    lo = (sq - hi.astype(F32)).astype(BF16)
    ms = _dot(hi, bd) + _dot(lo, bd)
    return o * lax.rsqrt(ms + EPS) * g * (gate * _sigmoid(gate))


def _ffn_kernel(x_ref, mod_ref, og_ref, gg_ref, or_ref, rg_ref, osb_ref, gng_ref, rng_ref, bd_ref,
                wout_ref, n2g_ref, wg_ref, wu_ref, wd_ref, fg_ref, o_ref, p_ref, *, final):
    d = D_MODEL
    x = x_ref[0]
    mod = mod_ref[0]
    g1, sh2, sc2, g2 = mod[:, 2 * d:3 * d], mod[:, 3 * d:4 * d], mod[:, 4 * d:5 * d], mod[:, 5 * d:6 * d]
    bd = bd_ref[...]
    o_gla = _head_norm_gate(og_ref[0], gng_ref[...], gg_ref[0], bd)
    o_ret = _head_norm_gate(or_ref[0], rng_ref[...], rg_ref[0], bd)
    o = jnp.concatenate([o_gla.astype(BF16), o_ret.astype(BF16), osb_ref[0]], axis=1)
    x1 = x + g1 * _dot(o, wout_ref[...])
    ms = jnp.mean(x1 * x1, axis=-1, keepdims=True)
    h = (x1 * lax.rsqrt(ms + EPS) * (n2g_ref[...] * (1.0 + sc2)) + sh2).astype(BF16)
    for c0 in range(0, D_FF, FF_CHUNK):
        a = _dot(h, wg_ref[:, c0:c0 + FF_CHUNK])
        u = _dot(h, wu_ref[:, c0:c0 + FF_CHUNK])
        p_ref[:, c0:c0 + FF_CHUNK] = (a * _sigmoid(a) * u).astype(BF16)
    x2 = x1 + g2 * _dot(p_ref[...], wd_ref[...])
    if final:
        ms = jnp.mean(x2 * x2, axis=-1, keepdims=True)
        x2 = x2 * lax.rsqrt(ms + EPS) * fg_ref[...]
    o_ref[0] = x2


def _out_ffn(x, mod, og, gg, orr, rg, osb, gng, rng, bd, wout, n2g, wg, wu, wd, fg, final):
    b, s, d = x.shape
    tm = min(TOKEN_TILE, s)
    tok = lambda width: pl.BlockSpec((1, tm, width), lambda bi, i: (bi, i, 0))
    const = lambda a: pl.BlockSpec(a.shape, lambda bi, i: tuple(0 for _ in a.shape),
                                   pipeline_mode=pl.Buffered(1))
    return pl.pallas_call(
        functools.partial(_ffn_kernel, final=final),
        grid=(b, s // tm),
        in_specs=[tok(d), pl.BlockSpec((1, 1, 6 * d), lambda bi, i: (bi, 0, 0)),
                  tok(GLA_V), tok(GLA_V), tok(RET_W), tok(RET_W), tok(SB_W),
                  const(gng), const(rng), const(bd), const(wout), const(n2g),
                  const(wg), const(wu), const(wd), const(fg)],
        out_specs=tok(d),
        out_shape=jax.ShapeDtypeStruct((b, s, d), F32),
        scratch_shapes=[pltpu.VMEM((tm, D_FF), BF16)],
        compiler_params=_params("arbitrary", "arbitrary"),
        name="out_projection_ffn",
    )(x, mod, og, gg, orr, rg, osb, gng, rng, bd, wout, n2g, wg, wu, wd, fg)


def _rope_tables(s):
    inv = ROPE_BASE ** (-np.arange(0, RET_DK, 2, dtype=np.float64) / RET_DK)
    ang = np.arange(s, dtype=np.float64)[:, None] * inv[None, :]
    cos = np.repeat(np.cos(ang), 2, axis=1)
    sin = np.repeat(np.sin(ang), 2, axis=1) * np.tile([-1.0, 1.0], RET_DK // 2)[None, :]
    reps = LANES // RET_DK
    return jnp.asarray(np.tile(cos, (1, reps)), F32), jnp.asarray(np.tile(sin, (1, reps)), F32)


def kernel(x, c, ada_w, ada_b, norm1_g, norm2_g, w_in, gla_wa2, gla_ba, gla_norm_g, ret_norm_g,
           w_out, ffn_wg, ffn_wu, ffn_wd, final_g):
    b, s, d = x.shape
    mods = _modulation(c, ada_w, ada_b)
    cos, sin = _rope_tables(s)
    gla_tables = _gla_tables()
    ret_tables = _ret_tables()
    hd = np.arange(GLA_V)
    bd = jnp.asarray((hd[:, None] // GLA_DV == hd[None, :] // GLA_DV) / float(GLA_DV), BF16)
    gr0 = 2 * GLA_QK + 2 * GLA_V
    for l in range(DEPTH):
        w = w_in[l]
        w_perm = jnp.concatenate(
            [w[:, :gr0], w[:, gr0 + GLA_RANK:], w[:, gr0:gr0 + GLA_RANK],
             jnp.zeros((d, IN_COLS_PADDED - C_GR - GLA_RANK), w.dtype)], axis=1).astype(BF16)
        wa2 = jnp.concatenate([gla_wa2[l], jnp.zeros((LANES - GLA_RANK, GLA_QK), F32)], axis=0).astype(BF16)
        mod = mods[l].reshape(b, 1, 6 * d)
        o_gla, gg, o_ret, rg, o_sb = _project_and_mix(
            x, mod, norm1_g[l].reshape(1, d), w_perm, wa2, gla_ba[l].reshape(1, GLA_QK), cos, sin,
            gla_tables, ret_tables)
        x = _out_ffn(x, mod, o_gla, gg, o_ret, rg, o_sb,
                     jnp.tile(gla_norm_g[l], GLA_HEADS).reshape(1, GLA_V),
                     jnp.tile(ret_norm_g[l], RET_HEADS).reshape(1, RET_W), bd,
                     w_out[l].astype(BF16), norm2_g[l].reshape(1, d),
                     ffn_wg[l].astype(BF16), ffn_wu[l].astype(BF16), ffn_wd[l].astype(BF16),
                     final_g.reshape(1, d), l == DEPTH - 1)
    return x
```

```python
import functools

import jax
import jax.numpy as jnp
import numpy as np
from jax import lax
from jax.experimental import pallas as pl
from jax.experimental.pallas import tpu as pltpu

F32 = jnp.float32
BF16 = jnp.bfloat16

D_MODEL = 1024
DEPTH = 2
GLA_HEADS, GLA_DK, GLA_DV, GLA_RANK = 4, 32, 64, 16
GLA_GATE_NORM = 16.0
RET_HEADS, RET_DK, RET_DV = 4, 64, 64
ROPE_BASE = 10000.0
SB_HEADS, SB_DK, SB_DV = 8, 64, 64
D_FF = 2816
EPS = 1e-6

GLA_QK = GLA_HEADS * GLA_DK
GLA_V = GLA_HEADS * GLA_DV
RET_W = RET_HEADS * RET_DK
SB_W = SB_HEADS * SB_DK
C_GQ, C_GK, C_GV, C_GG = 0, 128, 256, 512
C_RQ, C_RK, C_RV, C_RG = 768, 1024, 1280, 1536
C_SQ, C_SK, C_SV = 1792, 2304, 2816
C_GR = 3328
IN_COLS_PADDED = 3456

LANES = 128
GLA_CHUNK = 64
GLA_LEVELS = 6
RET_CHUNK = 256
SB_TQ = 512
SB_TK = 256
SB_UNDERFLOW_LOG2 = 160.0
SB_EXP_CLAMP = 64.0
TOKEN_TILE = 512
FF_CHUNK = 256
VMEM_LIMIT = 56 * 1024 * 1024


def _dot(a, b):
    return jnp.dot(a, b, preferred_element_type=F32)


def _dot_nt(a, b):
    return lax.dot_general(a, b, (((1,), (1,)), ((), ())), preferred_element_type=F32)


def _dot_tn(a, b):
    return lax.dot_general(a, b, (((0,), (0,)), ((), ())), preferred_element_type=F32)


def _sigmoid(x):
    return 1.0 / (1.0 + jnp.exp(-x))


def _softplus(x):
    return jnp.maximum(x, 0.0) + jnp.log(1.0 + jnp.exp(-jnp.abs(x)))


def _log2e(width):
    return 1.0 / jnp.log(jnp.full((1, width), 2.0, F32))


def _params(*sem):
    return pltpu.CompilerParams(dimension_semantics=sem, vmem_limit_bytes=VMEM_LIMIT)


def _mod_kernel(c_ref, w_ref, b_ref, o_ref):
    c = c_ref[...]
    cs = c * _sigmoid(c)
    o_ref[0] = jnp.dot(cs, w_ref[0], preferred_element_type=F32,
                       precision=lax.Precision.HIGHEST) + b_ref[0]


def _modulation(c, ada_w, ada_b):
    depth, d, n = ada_w.shape
    b = c.shape[0]
    tn = 1536
    return pl.pallas_call(
        _mod_kernel,
        grid=(depth, n // tn),
        in_specs=[pl.BlockSpec((b, d), lambda l, j: (0, 0)),
                  pl.BlockSpec((1, d, tn), lambda l, j: (l, 0, j)),
                  pl.BlockSpec((1, 1, tn), lambda l, j: (l, 0, j))],
        out_specs=pl.BlockSpec((1, b, tn), lambda l, j: (l, 0, j)),
        out_shape=jax.ShapeDtypeStruct((depth, b, n), F32),
        compiler_params=_params("arbitrary", "arbitrary"),
        name="adaln_modulation",
    )(c, ada_w, ada_b.reshape(depth, 1, n))


def _rope(t, cos, sin):
    lane = lax.broadcasted_iota(jnp.int32, cos.shape, 1)
    even = (lane & 1) == 0
    outs = []
    for c in range(t.shape[1] // LANES):
        tc = t[:, c * LANES:(c + 1) * LANES]
        nxt = pltpu.roll(tc, LANES - 1, 1)
        prv = pltpu.roll(tc, 1, 1)
        outs.append(tc * cos + jnp.where(even, nxt, prv) * sin)
    return jnp.concatenate(outs, axis=1)


def _proj_mix_kernel(x_ref, mod_ref, g_ref, w_ref, wa2_ref, ba_ref, cos_ref, sin_ref,
                     mcat_ref, lmask_ref, kmask_ref, vmask_ref, gsmask_ref,
                     dall_ref, xi_ref, zeta_ref, gc_ref, hmask_ref, rsmask_ref, tri_ref,
                     og_ref, gg_ref, or_ref, rg_ref, osb_ref,
                     gst_ref, rst_ref, skT_ref, sv_ref, acc_ref, c_ref):
    d = D_MODEL
    i = pl.program_id(1)

    @pl.when(pl.program_id(1) == 0)
    def _():
        gst_ref[...] = jnp.zeros_like(gst_ref)
        rst_ref[...] = jnp.zeros_like(rst_ref)

    x = x_ref[0]
    mod = mod_ref[0]
    sh1, sc1 = mod[:, 0:d], mod[:, d:2 * d]
    ms = jnp.mean(x * x, axis=-1, keepdims=True)
    h = (x * lax.rsqrt(ms + EPS) * (g_ref[...] * (1.0 + sc1)) + sh1).astype(BF16)

    def proj(a, width):
        return _dot(h, w_ref[:, a:a + width])

    sq = (proj(C_SQ, SB_W) * (_log2e(SB_W) * SB_DK ** -0.5)).astype(BF16)
    skT = proj(C_SK, SB_W).T.astype(BF16)
    sv_head = proj(C_SV, SB_W - LANES)
    sv_tail_gr = proj(C_GR - LANES, 2 * LANES)
    sv = jnp.concatenate([sv_head, sv_tail_gr[:, :LANES]], axis=1).astype(BF16)
    tokens = pl.ds(pl.multiple_of(i * SB_TQ, SB_TQ), SB_TQ)
    skT_ref[:, tokens] = skT
    sv_ref[tokens, :] = sv

    vals = {}
    tm = x.shape[0]
    hm = tm // 2
    assert hm == RET_CHUNK

    def proj_half(name, col, width, part, post=lambda t, rows: t):
        def run():
            rows = slice(part * hm, (part + 1) * hm)
            vals[name, part] = post(_dot(h[rows], w_ref[:, col:col + width]), rows)
        return run

    def rope_q(t, rows):
        return _rope(t, cos_ref[rows, :], sin_ref[rows, :])

    def rope_k(t, rows):
        return _rope(t, cos_ref[rows, :], sin_ref[rows, :]) * (RET_DK ** -0.5)

    def gla(part):
        def run():
            rows = slice(part * hm, (part + 1) * hm)
            gqk = vals["gqk", part]
            u = _dot(sv_tail_gr[rows, LANES:].astype(BF16), wa2_ref[...]) + ba_ref[...]
            la = -_softplus(-u) * (1.0 / GLA_GATE_NORM)
            gg_ref[0, rows, :] = vals["gg", part]
            og_ref[0, rows, :] = _gla_block(gqk[:, :GLA_QK] * (GLA_DK ** -0.5), gqk[:, GLA_QK:], vals["gv", part], la,
                                            mcat_ref, lmask_ref, kmask_ref, vmask_ref, gsmask_ref, gst_ref)
        return run

    def retention(part):
        def run():
            rows = slice(part * hm, (part + 1) * hm)
            rg_ref[0, rows, :] = vals["rg", part]
            or_ref[0, rows, :] = _ret_chunk(vals["rq", part], vals["rk", part], vals["rv", part], dall_ref,
                                            xi_ref, zeta_ref, gc_ref, hmask_ref, rsmask_ref, rst_ref)
        return run

    pieces = []
    for part in range(2):
        pieces += [proj_half("gqk", C_GQ, 2 * GLA_QK, part), proj_half("gv", C_GV, GLA_V, part),
                   proj_half("gg", C_GG, GLA_V, part), gla(part),
                   proj_half("rq", C_RQ, RET_W, part, rope_q), proj_half("rk", C_RK, RET_W, part, rope_k),
                   proj_half("rv", C_RV, RET_W, part), proj_half("rg", C_RG, RET_W, part), retention(part)]
    osb_ref[0] = _sb_block(i, sq, skT, sv, skT_ref, sv_ref, tri_ref, acc_ref, c_ref, interleave=pieces)


def _prep_w_in_kernel(w_ref, o_ref):
    w = w_ref[0]
    gr0 = 2 * GLA_QK + 2 * GLA_V
    o_ref[0] = jnp.concatenate(
        [w[:, :gr0], w[:, gr0 + GLA_RANK:], w[:, gr0:gr0 + GLA_RANK],
         jnp.zeros((w.shape[0], IN_COLS_PADDED - w.shape[1]), w.dtype)], axis=1).astype(BF16)


def _prep_w_in(w_in):
    depth, d, n = w_in.shape
    tr = LANES
    return pl.pallas_call(
        _prep_w_in_kernel,
        grid=(depth, d // tr),
        in_specs=[pl.BlockSpec((1, tr, n), lambda l, i: (l, i, 0))],
        out_specs=pl.BlockSpec((1, tr, IN_COLS_PADDED), lambda l, i: (l, i, 0)),
        out_shape=jax.ShapeDtypeStruct((depth, d, IN_COLS_PADDED), BF16),
        compiler_params=_params("arbitrary", "arbitrary"),
        name="w_in_prep",
    )(w_in)


def _project_and_mix(x, mod, g, w_all, layer, wa2, ba, cos, sin, gla_tables, ret_tables):
    b, s, d = x.shape
    tm = TOKEN_TILE
    assert tm == SB_TQ and s % tm == 0 and tm % GLA_CHUNK == 0 and tm % RET_CHUNK == 0
    tok = lambda width: pl.BlockSpec((1, tm, width), lambda bi, i: (bi, i, 0))
    const = lambda a: pl.BlockSpec(a.shape, lambda bi, i: tuple(0 for _ in a.shape))
    f32 = lambda width: jax.ShapeDtypeStruct((b, s, width), F32)
    r = np.arange(SB_TK)
    tri = jnp.asarray(r[:, None] > r[None, :], BF16)
    consts = (wa2, ba)
    return pl.pallas_call(
        _proj_mix_kernel,
        grid=(b, s // tm),
        in_specs=[tok(d), pl.BlockSpec((1, 1, 6 * d), lambda bi, i: (bi, 0, 0)), const(g),
                  pl.BlockSpec((None,) + w_all.shape[1:], lambda bi, i: (layer, 0, 0))]
                 + [const(a) for a in consts]
                 + [pl.BlockSpec((tm, LANES), lambda bi, i: (i, 0))] * 2
                 + [const(a) for a in gla_tables + ret_tables + (tri,)],
        out_specs=[tok(GLA_V), tok(GLA_V), tok(RET_W), tok(RET_W), tok(SB_W)],
        out_shape=[f32(GLA_V), f32(GLA_V), f32(RET_W), f32(RET_W),
                   jax.ShapeDtypeStruct((b, s, SB_W), BF16)],
        scratch_shapes=[pltpu.VMEM((GLA_V, GLA_QK), F32), pltpu.VMEM((RET_W, RET_W), F32),
                        pltpu.VMEM((SB_W, s), BF16), pltpu.VMEM((s, SB_W), BF16),
                        pltpu.VMEM((SB_HEADS, tm, LANES), F32), pltpu.VMEM((SB_HEADS, tm, LANES), F32)],
        compiler_params=_params("arbitrary", "arbitrary"),
        name="project_and_mix",
    )(x, mod, g, w_all, *consts, cos, sin, *gla_tables, *ret_tables, tri)


def _gla_tables():
    c, nl = GLA_CHUNK, GLA_LEVELS
    t = np.arange(c)
    rows = []
    for l in range(nl):
        w = 1 << l
        start = (t // w) * w
        rows.append((t[None, :] >= start[:, None]) & (t[None, :] <= t[:, None]))
    for l in range(nl):
        w = 1 << l
        end = (t // w + 1) * w - 1
        rows.append((t[None, :] > t[:, None]) & (t[None, :] <= end[:, None]))
    rows.append(t[None, :] <= t[:, None])
    mcat = np.concatenate(rows, 0).astype(np.float32)
    mcat3 = np.concatenate([mcat] * 3, 1)
    masks = []
    for l in range(nl):
        w = 1 << l
        same = (t[:, None] // (2 * w)) == (t[None, :] // (2 * w))
        upper = ((t // w) % 2 == 1)[:, None]
        lower = ((t // w) % 2 == 0)[None, :]
        masks.append(np.tile(same & upper & lower, (1, GLA_HEADS)))
    masks.append(np.tile(np.eye(c, dtype=bool), (1, GLA_HEADS)))
    lmask = np.stack(masks).astype(np.float32)
    r = np.arange(GLA_HEADS * c)
    kmask = (r[:, None] // c == np.arange(GLA_QK)[None, :] // GLA_DK).astype(np.float32)
    vmask = (r[:, None] // c == np.arange(GLA_V)[None, :] // GLA_DV).astype(np.float32)
    smask = (np.arange(GLA_V)[:, None] // GLA_DV == np.arange(GLA_QK)[None, :] // GLA_DK).astype(np.float32)
    return (jnp.asarray(mcat3, BF16), jnp.asarray(lmask), jnp.asarray(kmask),
            jnp.asarray(vmask), jnp.asarray(smask))


def _split3(x):
    hi = x.astype(BF16)
    r1 = x - hi.astype(F32)
    mid = r1.astype(BF16)
    lo = (r1 - mid.astype(F32)).astype(BF16)
    return hi, mid, lo


def _gla_block(q_all, k_all, v_all, la_all, mcat_ref, lmask_ref, kmask_ref, vmask_ref, smask_ref, st_ref):
    c, nl = GLA_CHUNK, GLA_LEVELS
    nchunk = q_all.shape[0] // c
    kmask, vmask, smask = kmask_ref[...], vmask_ref[...], smask_ref[...]
    la3 = jnp.concatenate([jnp.concatenate(_split3(la_all[ci * c:(ci + 1) * c]), axis=0)
                           for ci in range(nchunk)], axis=1)
    cs_all = _dot(mcat_ref[...], la3)
    e_all = jnp.exp(cs_all)
    outs = []
    for ci in range(nchunk):
        sl = slice(ci * c, (ci + 1) * c)
        q, k, v = q_all[sl], k_all[sl], v_all[sl]
        cs = cs_all[:, ci * GLA_QK:(ci + 1) * GLA_QK]
        e = e_all[:, ci * GLA_QK:(ci + 1) * GLA_QK]
        scores = jnp.zeros((c, GLA_HEADS * c), F32)
        for l in range(nl + 1):
            if l < nl:
                qt = q * e[l * c:(l + 1) * c]
                kt = k * e[(nl + l) * c:(nl + l + 1) * c]
            else:
                qt, kt = q, k
            krows = (jnp.concatenate([kt] * GLA_HEADS, axis=0) * kmask).astype(BF16)
            scores = scores + _dot_nt(qt.astype(BF16), krows) * lmask_ref[l]
        vbd = (jnp.concatenate([v] * GLA_HEADS, axis=0) * vmask).astype(BF16)
        o = _dot(scores.astype(BF16), vbd)
        bcum = cs[2 * nl * c:(2 * nl + 1) * c]
        st = st_ref[...]
        o = o + _dot_nt((q * e[2 * nl * c:(2 * nl + 1) * c]).astype(BF16), st.astype(BF16))
        blast = bcum[c - 1:c, :]
        kd = k * jnp.exp(blast - bcum)
        st_ref[...] = st * jnp.exp(blast) + _dot_tn(v.astype(BF16), kd.astype(BF16)) * smask
        outs.append(o)
    return jnp.concatenate(outs, axis=0)


def _ret_tables():
    c = RET_CHUNK
    gam = 1.0 - 2.0 ** (-5.0 - np.arange(RET_HEADS, dtype=np.float64))
    t = np.arange(c)
    rel = t[:, None] - t[None, :]
    dmat = np.where(rel[None] >= 0, gam[:, None, None] ** np.maximum(rel, 0)[None], 0.0)
    dall = np.concatenate(list(dmat), axis=1)
    xi = np.repeat((gam[None, :] ** (t[:, None] + 1.0)), RET_DK, axis=1)
    zeta = np.repeat((gam[None, :] ** (c - 1.0 - t[:, None])), RET_DK, axis=1)
    gc = np.repeat(gam ** c, RET_DK)[:, None] * np.ones((1, RET_W))
    r = np.arange(RET_HEADS * c)
    hmask = (r[:, None] // c == np.arange(RET_W)[None, :] // RET_DK)
    smask = (np.arange(RET_W)[:, None] // RET_DK == np.arange(RET_W)[None, :] // RET_DV)
    return tuple(jnp.asarray(a, F32) for a in (dall, xi, zeta, gc, hmask, smask))


def _ret_chunk(q, k, v, dall_ref, xi_ref, zeta_ref, gc_ref, hmask_ref, smask_ref, r_ref):
    hmask = hmask_ref[...]
    krows = (jnp.concatenate([k] * RET_HEADS, axis=0) * hmask).astype(BF16)
    scores = _dot_nt(q.astype(BF16), krows) * dall_ref[...]
    vbd = (jnp.concatenate([v] * RET_HEADS, axis=0) * hmask).astype(BF16)
    o = _dot(scores.astype(BF16), vbd)
    r = r_ref[...]
    o = o + _dot((q * xi_ref[...]).astype(BF16), r.astype(BF16))
    upd = _dot_tn((k * zeta_ref[...]).astype(BF16), v.astype(BF16))
    r_ref[...] = r * gc_ref[...] + upd * smask_ref[...]
    return o


def _sb_block(i, q, kT_new, v_new, kT_ref, v_ref, w_ref, acc_ref, c_ref, interleave):
    tq, tk = SB_TQ, SB_TK
    npairs = SB_HEADS // 2
    lane = lax.broadcasted_iota(jnp.int32, (tq, LANES), 1)
    qs = []
    for pr in range(npairs):
        q2 = q[:, pr * LANES:(pr + 1) * LANES]
        zero = jnp.zeros_like(q2)
        qs += [jnp.where(lane < SB_DK, q2, zero), jnp.where(lane >= SB_DK, q2, zero)]
    acc_ref[...] = jnp.zeros_like(acc_ref)
    c_ref[...] = jnp.zeros_like(c_ref)
    w = w_ref[...]
    log2e = _log2e(tk)

    def tile(j, r0, r1, mask_rows, local=None, valid=None, heads=range(SB_HEADS)):
        start = pl.multiple_of(j * tk, tk)
        sections = [(r0, r0 + mask_rows, True), (r0 + mask_rows, r1, False)]
        for pr in sorted({hd // 2 for hd in heads}):
            cols = slice(pr * LANES, (pr + 1) * LANES)
            if local is None:
                kT2 = kT_ref[cols, pl.ds(start, tk)]
                v2 = v_ref[pl.ds(start, tk), cols]
            else:
                kT2 = kT_new[cols, local * tk:(local + 1) * tk]
                v2 = v_new[local * tk:(local + 1) * tk, cols]
            for hd in (h for h in (2 * pr, 2 * pr + 1) if h in heads):
                z = _dot(qs[hd][r0:r1], kT2)
                for a, b, masked in sections:
                    if a == b:
                        continue
                    zs = z[a - r0:b - r0]
                    c = c_ref[hd, a:b, :]
                    sp = jnp.maximum(jnp.log(1.0 + jnp.exp2(jnp.minimum(zs, SB_EXP_CLAMP))) * log2e, zs)
                    if masked:
                        key_s = lax.broadcasted_iota(jnp.int32, zs.shape, 1) + start
                        query_t = lax.broadcasted_iota(jnp.int32, zs.shape, 0) + (i * tq + a)
                        causal = key_s < query_t
                        sp = jnp.where(causal, sp, 0.0)
                    if valid is not None:
                        sp = jnp.where(valid, sp, 0.0)
                    r = _dot(sp.astype(BF16), w)
                    p = jnp.exp2((zs - sp) - (r + jnp.concatenate([c] * (tk // LANES), axis=1)))
                    if masked:
                        p = jnp.where(causal, p, 0.0)
                    if valid is not None:
                        p = jnp.where(valid, p, 0.0)
                    c_ref[hd, a:b, :] = c + jnp.broadcast_to(r[:, 0:1] + sp[:, 0:1], c.shape)
                    acc_ref[hd, a:b, :] += _dot(p.astype(BF16), v2)

    nk = tq // tk
    half = tq // 2
    fillers = list(interleave)
    for hd in range(SB_HEADS):
        for d in reversed(range(nk)):
            tile(i * nk + d, d * tk, tq, tk, local=d, heads=(hd,))
            if fillers:
                fillers.pop(0)()
        tile(jnp.maximum(i * nk - 1, 0), 0, half, 0, valid=i > 0, heads=(hd,))
        if fillers:
            fillers.pop(0)()
    assert not fillers

    def mins():
        c = c_ref[...]
        return jnp.min(c[:, :half]), jnp.min(c[:, half:])

    def cond(st):
        return jnp.logical_and(st[0] < i * nk, jnp.minimum(st[1], st[2]) < SB_UNDERFLOW_LOG2)

    def body(st):
        j = i * nk - 1 - st[0]
        lower_done = st[2] >= SB_UNDERFLOW_LOG2

        def first():
            @pl.when(jnp.logical_not(lower_done))
            def _():
                tile(j, half, tq, 0)

        def later():
            lax.cond(lower_done, lambda: tile(j, 0, half, 0), lambda: tile(j, 0, tq, 0))

        lax.cond(st[0] == 0, first, later)
        return (st[0] + 1,) + mins()

    lax.while_loop(cond, body, (jnp.int32(0),) + mins())
    return jnp.concatenate(
        [jnp.where(lane < SB_DK, acc_ref[2 * pr], acc_ref[2 * pr + 1]).astype(BF16) for pr in range(npairs)], axis=1)


def _head_norm_gate(o, g, gate, bd):
    ms = _dot((o * o).astype(BF16), bd)
    return o * lax.rsqrt(ms + EPS) * g * (gate * _sigmoid(gate))


def _ffn_kernel(x_ref, mod_ref, og_ref, gg_ref, or_ref, rg_ref, osb_ref, gng_ref, rng_ref, bd_ref,
                wout_ref, n2g_ref, wg_ref, wu_ref, wd_ref, fg_ref, o_ref, p_ref, *, final):
    d = D_MODEL
    x = x_ref[0]
    mod = mod_ref[0]
    g1, sh2, sc2, g2 = mod[:, 2 * d:3 * d], mod[:, 3 * d:4 * d], mod[:, 4 * d:5 * d], mod[:, 5 * d:6 * d]
    bd = bd_ref[...]
    o_gla = _head_norm_gate(og_ref[0], gng_ref[...], gg_ref[0], bd)
    o_ret = _head_norm_gate(or_ref[0], rng_ref[...], rg_ref[0], bd)
    o = jnp.concatenate([o_gla.astype(BF16), o_ret.astype(BF16), osb_ref[0]], axis=1)
    x1 = x + g1 * _dot(o, wout_ref[...])
    ms = jnp.mean(x1 * x1, axis=-1, keepdims=True)
    h = (x1 * lax.rsqrt(ms + EPS) * (n2g_ref[...] * (1.0 + sc2)) + sh2).astype(BF16)
    for c0 in range(0, D_FF, FF_CHUNK):
        a = _dot(h, wg_ref[:, c0:c0 + FF_CHUNK])
        u = _dot(h, wu_ref[:, c0:c0 + FF_CHUNK])
        p_ref[:, c0:c0 + FF_CHUNK] = (a * _sigmoid(a) * u).astype(BF16)
    x2 = x1 + g2 * _dot(p_ref[...], wd_ref[...])
    if final:
        ms = jnp.mean(x2 * x2, axis=-1, keepdims=True)
        x2 = x2 * lax.rsqrt(ms + EPS) * fg_ref[...]
    o_ref[0] = x2


def _out_ffn(x, mod, og, gg, orr, rg, osb, gng, rng, bd, wout, n2g, wg, wu, wd, fg, layer, final):
    b, s, d = x.shape
    tm = min(TOKEN_TILE, s)
    tok = lambda width: pl.BlockSpec((1, tm, width), lambda bi, i: (bi, i, 0))
    const = lambda a: pl.BlockSpec(a.shape, lambda bi, i: tuple(0 for _ in a.shape),
                                   pipeline_mode=pl.Buffered(1))
    weight = lambda a: pl.BlockSpec((None,) + a.shape[1:], lambda bi, i: (layer, 0, 0),
                                    pipeline_mode=pl.Buffered(1))
    return pl.pallas_call(
        functools.partial(_ffn_kernel, final=final),
        grid=(b, s // tm),
        in_specs=[tok(d), pl.BlockSpec((1, 1, 6 * d), lambda bi, i: (bi, 0, 0)),
                  tok(GLA_V), tok(GLA_V), tok(RET_W), tok(RET_W), tok(SB_W),
                  const(gng), const(rng), const(bd), weight(wout), const(n2g),
                  weight(wg), weight(wu), weight(wd), const(fg)],
        out_specs=tok(d),
        out_shape=jax.ShapeDtypeStruct((b, s, d), F32),
        scratch_shapes=[pltpu.VMEM((tm, D_FF), BF16)],
        compiler_params=_params("arbitrary", "arbitrary"),
        name="out_projection_ffn",
    )(x, mod, og, gg, orr, rg, osb, gng, rng, bd, wout, n2g, wg, wu, wd, fg)


def _rope_tables(s):
    inv = ROPE_BASE ** (-np.arange(0, RET_DK, 2, dtype=np.float64) / RET_DK)
    ang = np.arange(s, dtype=np.float64)[:, None] * inv[None, :]
    cos = np.repeat(np.cos(ang), 2, axis=1)
    sin = np.repeat(np.sin(ang), 2, axis=1) * np.tile([-1.0, 1.0], RET_DK // 2)[None, :]
    reps = LANES // RET_DK
    return jnp.asarray(np.tile(cos, (1, reps)), F32), jnp.asarray(np.tile(sin, (1, reps)), F32)


def kernel(x, c, ada_w, ada_b, norm1_g, norm2_g, w_in, gla_wa2, gla_ba, gla_norm_g, ret_norm_g,
           w_out, ffn_wg, ffn_wu, ffn_wd, final_g):
    b, s, d = x.shape
    mods = _modulation(c, ada_w, ada_b)
    cos, sin = _rope_tables(s)
    gla_tables = _gla_tables()
    ret_tables = _ret_tables()
    hd = np.arange(GLA_V)
    bd = jnp.asarray((hd[:, None] // GLA_DV == hd[None, :] // GLA_DV) / float(GLA_DV), BF16)
    w_in_b = _prep_w_in(w_in)
    w_out_b, wg_b, wu_b, wd_b = (a.astype(BF16) for a in (w_out, ffn_wg, ffn_wu, ffn_wd))
    for l in range(DEPTH):
        wa2 = jnp.concatenate([gla_wa2[l], jnp.zeros((LANES - GLA_RANK, GLA_QK), F32)], axis=0).astype(BF16)
        mod = mods[l].reshape(b, 1, 6 * d)
        o_gla, gg, o_ret, rg, o_sb = _project_and_mix(
            x, mod, norm1_g[l].reshape(1, d), w_in_b, l, wa2, gla_ba[l].reshape(1, GLA_QK), cos, sin,
            gla_tables, ret_tables)
        x = _out_ffn(x, mod, o_gla, gg, o_ret, rg, o_sb,
                     jnp.tile(gla_norm_g[l], GLA_HEADS).reshape(1, GLA_V),
                     jnp.tile(ret_norm_g[l], RET_HEADS).reshape(1, RET_W), bd,
                     w_out_b, norm2_g[l].reshape(1, d), wg_b, wu_b, wd_b,
                     final_g.reshape(1, d), l, l == DEPTH - 1)
    return x
```

```python
import functools

import jax
import jax.numpy as jnp
import numpy as np
from jax import lax
from jax.experimental import pallas as pl
from jax.experimental.pallas import tpu as pltpu

F32 = jnp.float32
BF16 = jnp.bfloat16

D_MODEL = 1024
DEPTH = 2
GLA_HEADS, GLA_DK, GLA_DV, GLA_RANK = 4, 32, 64, 16
GLA_GATE_NORM = 16.0
RET_HEADS, RET_DK, RET_DV = 4, 64, 64
ROPE_BASE = 10000.0
SB_HEADS, SB_DK, SB_DV = 8, 64, 64
D_FF = 2816
EPS = 1e-6

GLA_QK = GLA_HEADS * GLA_DK
GLA_V = GLA_HEADS * GLA_DV
RET_W = RET_HEADS * RET_DK
SB_W = SB_HEADS * SB_DK
C_GQ, C_GK, C_GV, C_GG = 0, 128, 256, 512
C_RQ, C_RK, C_RV, C_RG = 768, 1024, 1280, 1536
C_SQ, C_SK, C_SV = 1792, 2304, 2816
C_GR = 3328
IN_COLS_PADDED = 3456

LANES = 128
GLA_CHUNK = 64
GLA_LEVELS = 6
RET_CHUNK = 256
SB_TQ = 512
SB_TK = 256
SB_UNDERFLOW_LOG2 = 160.0
SB_EXP_CLAMP = 64.0
TOKEN_TILE = 512
FF_CHUNK = 256
VMEM_LIMIT = 56 * 1024 * 1024


def _dot(a, b):
    return jnp.dot(a, b, preferred_element_type=F32)


def _dot_nt(a, b):
    return lax.dot_general(a, b, (((1,), (1,)), ((), ())), preferred_element_type=F32)


def _dot_tn(a, b):
    return lax.dot_general(a, b, (((0,), (0,)), ((), ())), preferred_element_type=F32)


def _sigmoid(x):
    return 1.0 / (1.0 + jnp.exp(-x))


def _softplus(x):
    return jnp.maximum(x, 0.0) + jnp.log(1.0 + jnp.exp(-jnp.abs(x)))


def _log2e(width):
    return 1.0 / jnp.log(jnp.full((1, width), 2.0, F32))


def _params(*sem):
    return pltpu.CompilerParams(dimension_semantics=sem, vmem_limit_bytes=VMEM_LIMIT)


def _mod_kernel(c_ref, w_ref, b_ref, o_ref):
    c = c_ref[...]
    cs = c * _sigmoid(c)
    o_ref[0] = jnp.dot(cs, w_ref[0], preferred_element_type=F32,
                       precision=lax.Precision.HIGHEST) + b_ref[0]


def _modulation(c, ada_w, ada_b):
    depth, d, n = ada_w.shape
    b = c.shape[0]
    tn = 1536
    return pl.pallas_call(
        _mod_kernel,
        grid=(depth, n // tn),
        in_specs=[pl.BlockSpec((b, d), lambda l, j: (0, 0)),
                  pl.BlockSpec((1, d, tn), lambda l, j: (l, 0, j)),
                  pl.BlockSpec((1, 1, tn), lambda l, j: (l, 0, j))],
        out_specs=pl.BlockSpec((1, b, tn), lambda l, j: (l, 0, j)),
        out_shape=jax.ShapeDtypeStruct((depth, b, n), F32),
        compiler_params=_params("arbitrary", "arbitrary"),
        name="adaln_modulation",
    )(c, ada_w, ada_b.reshape(depth, 1, n))


def _rope(t, cos, sin):
    lane = lax.broadcasted_iota(jnp.int32, cos.shape, 1)
    even = (lane & 1) == 0
    outs = []
    for c in range(t.shape[1] // LANES):
        tc = t[:, c * LANES:(c + 1) * LANES]
        nxt = pltpu.roll(tc, LANES - 1, 1)
        prv = pltpu.roll(tc, 1, 1)
        outs.append(tc * cos + jnp.where(even, nxt, prv) * sin)
    return jnp.concatenate(outs, axis=1)


def _proj_mix_kernel(x_ref, mod_ref, g_ref, w_ref, wa2_ref, ba_ref, cos_ref, sin_ref,
                     mcat_ref, lmask_ref, kmask_ref, vmask_ref, gsmask_ref,
                     dall_ref, xi_ref, zeta_ref, gc_ref, hmask_ref, rsmask_ref, tri_ref,
                     og_ref, gg_ref, or_ref, rg_ref, osb_ref,
                     gst_ref, rst_ref, skT_ref, sv_ref, acc_ref, c_ref):
    d = D_MODEL
    i = pl.program_id(1)

    @pl.when(pl.program_id(1) == 0)
    def _():
        gst_ref[...] = jnp.zeros_like(gst_ref)
        rst_ref[...] = jnp.zeros_like(rst_ref)

    x = x_ref[0]
    mod = mod_ref[0]
    sh1, sc1 = mod[:, 0:d], mod[:, d:2 * d]
    ms = jnp.mean(x * x, axis=-1, keepdims=True)
    h = (x * lax.rsqrt(ms + EPS) * (g_ref[...] * (1.0 + sc1)) + sh1).astype(BF16)

    def proj(a, width):
        return _dot(h, w_ref[:, a:a + width])

    sq = (proj(C_SQ, SB_W) * (_log2e(SB_W) * SB_DK ** -0.5)).astype(BF16)
    skT = proj(C_SK, SB_W).T.astype(BF16)
    sv_head = proj(C_SV, SB_W - LANES)
    sv_tail_gr = proj(C_GR - LANES, 2 * LANES)
    sv = jnp.concatenate([sv_head, sv_tail_gr[:, :LANES]], axis=1).astype(BF16)
    tokens = pl.ds(pl.multiple_of(i * SB_TQ, SB_TQ), SB_TQ)
    skT_ref[:, tokens] = skT
    sv_ref[tokens, :] = sv

    vals = {}
    tm = x.shape[0]
    hm = tm // 2
    assert hm == RET_CHUNK

    def proj_half(name, col, width, part, post=lambda t, rows: t):
        def run():
            rows = slice(part * hm, (part + 1) * hm)
            vals[name, part] = post(_dot(h[rows], w_ref[:, col:col + width]), rows)
        return run

    def rope_q(t, rows):
        return _rope(t, cos_ref[rows, :], sin_ref[rows, :])

    def rope_k(t, rows):
        return _rope(t, cos_ref[rows, :], sin_ref[rows, :]) * (RET_DK ** -0.5)

    def gla(part):
        def run():
            rows = slice(part * hm, (part + 1) * hm)
            gqk = vals["gqk", part]
            u = _dot(sv_tail_gr[rows, LANES:].astype(BF16), wa2_ref[...]) + ba_ref[...]
            la = -_softplus(-u) * (1.0 / GLA_GATE_NORM)
            gg_ref[0, rows, :] = vals["gg", part]
            og_ref[0, rows, :] = _gla_block(gqk[:, :GLA_QK] * (GLA_DK ** -0.5), gqk[:, GLA_QK:], vals["gv", part], la,
                                            mcat_ref, lmask_ref, kmask_ref, vmask_ref, gsmask_ref, gst_ref)
        return run

    def retention(part):
        def run():
            rows = slice(part * hm, (part + 1) * hm)
            rg_ref[0, rows, :] = vals["rg", part]
            or_ref[0, rows, :] = _ret_chunk(vals["rq", part], vals["rk", part], vals["rv", part], dall_ref,
                                            xi_ref, zeta_ref, gc_ref, hmask_ref, rsmask_ref, rst_ref)
        return run

    pieces = []
    for part in range(2):
        pieces += [proj_half("gqk", C_GQ, 2 * GLA_QK, part), proj_half("gv", C_GV, GLA_V, part),
                   proj_half("gg", C_GG, GLA_V, part), gla(part),
                   proj_half("rq", C_RQ, RET_W, part, rope_q), proj_half("rk", C_RK, RET_W, part, rope_k),
                   proj_half("rv", C_RV, RET_W, part), proj_half("rg", C_RG, RET_W, part), retention(part)]
    osb_ref[0] = _sb_block(i, sq, skT, sv, skT_ref, sv_ref, tri_ref, acc_ref, c_ref, interleave=pieces)


def _prep_w_in_kernel(w_ref, o_ref):
    w = w_ref[0]
    gr0 = 2 * GLA_QK + 2 * GLA_V
    o_ref[0] = jnp.concatenate(
        [w[:, :gr0], w[:, gr0 + GLA_RANK:], w[:, gr0:gr0 + GLA_RANK],
         jnp.zeros((w.shape[0], IN_COLS_PADDED - w.shape[1]), w.dtype)], axis=1).astype(BF16)


def _prep_w_in(w_in):
    depth, d, n = w_in.shape
    tr = LANES
    return pl.pallas_call(
        _prep_w_in_kernel,
        grid=(depth, d // tr),
        in_specs=[pl.BlockSpec((1, tr, n), lambda l, i: (l, i, 0))],
        out_specs=pl.BlockSpec((1, tr, IN_COLS_PADDED), lambda l, i: (l, i, 0)),
        out_shape=jax.ShapeDtypeStruct((depth, d, IN_COLS_PADDED), BF16),
        compiler_params=_params("arbitrary", "arbitrary"),
        name="w_in_prep",
    )(w_in)


def _project_and_mix(x, mod, g, w_all, layer, wa2, ba, cos, sin, gla_tables, ret_tables):
    b, s, d = x.shape
    tm = TOKEN_TILE
    assert tm == SB_TQ and s % tm == 0 and tm % GLA_CHUNK == 0 and tm % RET_CHUNK == 0
    tok = lambda width: pl.BlockSpec((1, tm, width), lambda bi, i: (bi, i, 0))
    const = lambda a: pl.BlockSpec(a.shape, lambda bi, i: tuple(0 for _ in a.shape))
    f32 = lambda width: jax.ShapeDtypeStruct((b, s, width), F32)
    r = np.arange(SB_TK)
    tri = jnp.asarray(r[:, None] > r[None, :], BF16)
    consts = (wa2, ba)
    return pl.pallas_call(
        _proj_mix_kernel,
        grid=(b, s // tm),
        in_specs=[tok(d), pl.BlockSpec((1, 1, 6 * d), lambda bi, i: (bi, 0, 0)), const(g),
                  pl.BlockSpec((None,) + w_all.shape[1:], lambda bi, i: (layer, 0, 0))]
                 + [const(a) for a in consts]
                 + [pl.BlockSpec((tm, LANES), lambda bi, i: (i, 0))] * 2
                 + [const(a) for a in gla_tables + ret_tables + (tri,)],
        out_specs=[tok(GLA_V), tok(GLA_V), tok(RET_W), tok(RET_W), tok(SB_W)],
        out_shape=[f32(GLA_V), f32(GLA_V), f32(RET_W), f32(RET_W),
                   jax.ShapeDtypeStruct((b, s, SB_W), BF16)],
        scratch_shapes=[pltpu.VMEM((GLA_V, GLA_QK), F32), pltpu.VMEM((RET_W, RET_W), F32),
                        pltpu.VMEM((SB_W, s), BF16), pltpu.VMEM((s, SB_W), BF16),
                        pltpu.VMEM((SB_HEADS, tm, LANES), F32), pltpu.VMEM((SB_HEADS, tm, LANES), F32)],
        compiler_params=_params("arbitrary", "arbitrary"),
        name="project_and_mix",
    )(x, mod, g, w_all, *consts, cos, sin, *gla_tables, *ret_tables, tri)


def _gla_tables():
    c, nl = GLA_CHUNK, GLA_LEVELS
    t = np.arange(c)
    rows = []
    for l in range(nl):
        w = 1 << l
        start = (t // w) * w
        rows.append((t[None, :] >= start[:, None]) & (t[None, :] <= t[:, None]))
    for l in range(nl):
        w = 1 << l
        end = (t // w + 1) * w - 1
        rows.append((t[None, :] > t[:, None]) & (t[None, :] <= end[:, None]))
    rows.append(t[None, :] <= t[:, None])
    mcat = np.concatenate(rows, 0).astype(np.float32)
    mcat3 = np.concatenate([mcat] * 3, 1)
    masks = []
    for l in range(nl):
        w = 1 << l
        same = (t[:, None] // (2 * w)) == (t[None, :] // (2 * w))
        upper = ((t // w) % 2 == 1)[:, None]
        lower = ((t // w) % 2 == 0)[None, :]
        masks.append(np.tile(same & upper & lower, (1, GLA_HEADS)))
    masks.append(np.tile(np.eye(c, dtype=bool), (1, GLA_HEADS)))
    lmask = np.stack(masks).astype(np.float32)
    r = np.arange(GLA_HEADS * c)
    kmask = (r[:, None] // c == np.arange(GLA_QK)[None, :] // GLA_DK).astype(np.float32)
    vmask = (r[:, None] // c == np.arange(GLA_V)[None, :] // GLA_DV).astype(np.float32)
    smask = (np.arange(GLA_V)[:, None] // GLA_DV == np.arange(GLA_QK)[None, :] // GLA_DK).astype(np.float32)
    return (jnp.asarray(mcat3, BF16), jnp.asarray(lmask), jnp.asarray(kmask),
            jnp.asarray(vmask), jnp.asarray(smask))


def _split3(x):
    hi = x.astype(BF16)
    r1 = x - hi.astype(F32)
    mid = r1.astype(BF16)
    lo = (r1 - mid.astype(F32)).astype(BF16)
    return hi, mid, lo


def _gla_block(q_all, k_all, v_all, la_all, mcat_ref, lmask_ref, kmask_ref, vmask_ref, smask_ref, st_ref):
    c, nl = GLA_CHUNK, GLA_LEVELS
    nchunk = q_all.shape[0] // c
    kmask, vmask, smask = kmask_ref[...], vmask_ref[...], smask_ref[...]
    la3 = jnp.concatenate([jnp.concatenate(_split3(la_all[ci * c:(ci + 1) * c]), axis=0)
                           for ci in range(nchunk)], axis=1)
    cs_all = _dot(mcat_ref[...], la3)
    e_all = jnp.exp(cs_all)
    chunks = range(nchunk)
    q = [q_all[ci * c:(ci + 1) * c] for ci in chunks]
    k = [k_all[ci * c:(ci + 1) * c] for ci in chunks]
    v = [v_all[ci * c:(ci + 1) * c] for ci in chunks]
    cs = [cs_all[:, ci * GLA_QK:(ci + 1) * GLA_QK] for ci in chunks]
    e = [e_all[:, ci * GLA_QK:(ci + 1) * GLA_QK] for ci in chunks]
    scores = [jnp.zeros((c, GLA_HEADS * c), F32) for _ in chunks]
    for l in range(nl + 1):
        for ci in chunks:
            if l < nl:
                qt = q[ci] * e[ci][l * c:(l + 1) * c]
                kt = k[ci] * e[ci][(nl + l) * c:(nl + l + 1) * c]
            else:
                qt, kt = q[ci], k[ci]
            krows = (jnp.concatenate([kt] * GLA_HEADS, axis=0) * kmask).astype(BF16)
            scores[ci] = scores[ci] + _dot_nt(qt.astype(BF16), krows) * lmask_ref[l]
    intra = [_dot(scores[ci].astype(BF16), (jnp.concatenate([v[ci]] * GLA_HEADS, axis=0) * vmask).astype(BF16))
             for ci in chunks]
    bcum = [cs[ci][2 * nl * c:(2 * nl + 1) * c] for ci in chunks]
    qd = [(q[ci] * e[ci][2 * nl * c:(2 * nl + 1) * c]).astype(BF16) for ci in chunks]
    upd = [_dot_tn(v[ci].astype(BF16), (k[ci] * jnp.exp(bcum[ci][c - 1:c, :] - bcum[ci])).astype(BF16)) * smask
           for ci in chunks]
    outs = []
    st = st_ref[...]
    for ci in chunks:
        outs.append(intra[ci] + _dot_nt(qd[ci], st.astype(BF16)))
        st = st * jnp.exp(bcum[ci][c - 1:c, :]) + upd[ci]
    st_ref[...] = st
    return jnp.concatenate(outs, axis=0)


def _ret_tables():
    c = RET_CHUNK
    gam = 1.0 - 2.0 ** (-5.0 - np.arange(RET_HEADS, dtype=np.float64))
    t = np.arange(c)
    rel = t[:, None] - t[None, :]
    dmat = np.where(rel[None] >= 0, gam[:, None, None] ** np.maximum(rel, 0)[None], 0.0)
    dall = np.concatenate(list(dmat), axis=1)
    xi = np.repeat((gam[None, :] ** (t[:, None] + 1.0)), RET_DK, axis=1)
    zeta = np.repeat((gam[None, :] ** (c - 1.0 - t[:, None])), RET_DK, axis=1)
    gc = np.repeat(gam ** c, RET_DK)[:, None] * np.ones((1, RET_W))
    r = np.arange(RET_HEADS * c)
    hmask = (r[:, None] // c == np.arange(RET_W)[None, :] // RET_DK)
    smask = (np.arange(RET_W)[:, None] // RET_DK == np.arange(RET_W)[None, :] // RET_DV)
    return tuple(jnp.asarray(a, F32) for a in (dall, xi, zeta, gc, hmask, smask))


def _ret_chunk(q, k, v, dall_ref, xi_ref, zeta_ref, gc_ref, hmask_ref, smask_ref, r_ref):
    hmask = hmask_ref[...]
    krows = (jnp.concatenate([k] * RET_HEADS, axis=0) * hmask).astype(BF16)
    scores = _dot_nt(q.astype(BF16), krows) * dall_ref[...]
    vbd = (jnp.concatenate([v] * RET_HEADS, axis=0) * hmask).astype(BF16)
    o = _dot(scores.astype(BF16), vbd)
    r = r_ref[...]
    o = o + _dot((q * xi_ref[...]).astype(BF16), r.astype(BF16))
    upd = _dot_tn((k * zeta_ref[...]).astype(BF16), v.astype(BF16))
    r_ref[...] = r * gc_ref[...] + upd * smask_ref[...]
    return o


def _sb_block(i, q, kT_new, v_new, kT_ref, v_ref, w_ref, acc_ref, c_ref, interleave):
    tq, tk = SB_TQ, SB_TK
    npairs = SB_HEADS // 2
    lane = lax.broadcasted_iota(jnp.int32, (tq, LANES), 1)
    qs = []
    for pr in range(npairs):
        q2 = q[:, pr * LANES:(pr + 1) * LANES]
        zero = jnp.zeros_like(q2)
        qs += [jnp.where(lane < SB_DK, q2, zero), jnp.where(lane >= SB_DK, q2, zero)]
    w = w_ref[...]

    def softplus2(t):
        return jnp.maximum(jnp.log(1.0 + jnp.exp2(jnp.minimum(t, SB_EXP_CLAMP))) * _log2e(t.shape[1]), t)

    def masked_softplus(zs):
        hq = tk // 2
        below = (lax.broadcasted_iota(jnp.int32, (hq, hq), 1) < lax.broadcasted_iota(jnp.int32, (hq, hq), 0))
        z_ul, z_ll, z_lr = zs[:hq, :hq], zs[hq:, :hq], zs[hq:, hq:]
        sp_ul = jnp.where(below, softplus2(z_ul), 0.0)
        sp_ll = softplus2(z_ll)
        sp_lr = jnp.where(below, softplus2(z_lr), 0.0)
        hidden = jnp.zeros((hq, hq), F32)
        sp = jnp.concatenate([jnp.concatenate([sp_ul, hidden], axis=1),
                              jnp.concatenate([sp_ll, sp_lr], axis=1)], axis=0)

        def p_of(t):
            p_ul = jnp.where(below, jnp.exp2((z_ul - sp_ul) - t[:hq, :hq]), 0.0)
            p_ll = jnp.exp2((z_ll - sp_ll) - t[hq:, :hq])
            p_lr = jnp.where(below, jnp.exp2((z_lr - sp_lr) - t[hq:, hq:]), 0.0)
            return jnp.concatenate([jnp.concatenate([p_ul, hidden], axis=1),
                                    jnp.concatenate([p_ll, p_lr], axis=1)], axis=0)

        return sp, p_of

    def tile(j, r0, r1, mask_rows, local=None, valid=None, heads=range(SB_HEADS)):
        assert mask_rows in (0, tk)
        start = pl.multiple_of(j * tk, tk)
        sections = [(r0, r0 + mask_rows, True), (r0 + mask_rows, r1, False)]
        for pr in sorted({hd // 2 for hd in heads}):
            cols = slice(pr * LANES, (pr + 1) * LANES)
            if local is None:
                kT2 = kT_ref[cols, pl.ds(start, tk)]
                v2 = v_ref[pl.ds(start, tk), cols]
            else:
                kT2 = kT_new[cols, local * tk:(local + 1) * tk]
                v2 = v_new[local * tk:(local + 1) * tk, cols]
            if valid is not None:
                v2 = jnp.where(valid, v2, jnp.zeros_like(v2))
            for hd in (h for h in (2 * pr, 2 * pr + 1) if h in heads):
                z = _dot(qs[hd][r0:r1], kT2)
                for a, b, masked in sections:
                    if a == b:
                        continue
                    zs = z[a - r0:b - r0]
                    if masked:
                        sp, p_of = masked_softplus(zs)
                        r = _dot(sp.astype(BF16), w)
                        c_ref[hd, a:b, :] = jnp.broadcast_to(r[:, 0:1] + sp[:, 0:1], (b - a, LANES))
                        acc_ref[hd, a:b, :] = _dot(p_of(r).astype(BF16), v2)
                        continue
                    c = c_ref[hd, a:b, :]
                    sp = softplus2(zs)
                    r = _dot(sp.astype(BF16), w)
                    p = jnp.exp2((zs - sp) - (r + jnp.concatenate([c] * (tk // LANES), axis=1)))
                    total = r[:, 0:1] + sp[:, 0:1]
                    if valid is not None:
                        total = jnp.where(valid, total, 0.0)
                    c_ref[hd, a:b, :] = c + jnp.broadcast_to(total, c.shape)
                    acc_ref[hd, a:b, :] += _dot(p.astype(BF16), v2)

    nk = tq // tk
    half = tq // 2
    fillers = list(interleave)
    for hd in range(SB_HEADS):
        for d in reversed(range(nk)):
            tile(i * nk + d, d * tk, tq, tk, local=d, heads=(hd,))
            if fillers:
                fillers.pop(0)()
        tile(jnp.maximum(i * nk - 1, 0), 0, half, 0, valid=i > 0, heads=(hd,))
        if fillers:
            fillers.pop(0)()
    assert not fillers

    def mins():
        c = c_ref[...]
        return jnp.min(c[:, :half]), jnp.min(c[:, half:])

    def cond(st):
        return jnp.logical_and(st[0] < i * nk, jnp.minimum(st[1], st[2]) < SB_UNDERFLOW_LOG2)

    def body(st):
        j = i * nk - 1 - st[0]
        lower_done = st[2] >= SB_UNDERFLOW_LOG2

        def first():
            @pl.when(jnp.logical_not(lower_done))
            def _():
                tile(j, half, tq, 0)

        def later():
            lax.cond(lower_done, lambda: tile(j, 0, half, 0), lambda: tile(j, 0, tq, 0))

        lax.cond(st[0] == 0, first, later)
        return (st[0] + 1,) + mins()

    lax.while_loop(cond, body, (jnp.int32(0),) + mins())
    return jnp.concatenate(
        [jnp.where(lane < SB_DK, acc_ref[2 * pr], acc_ref[2 * pr + 1]).astype(BF16) for pr in range(npairs)], axis=1)


def _head_norm_gate(o, g, gate, bd):
    ms = _dot((o * o).astype(BF16), bd)
    return o * lax.rsqrt(ms + EPS) * g * (gate * _sigmoid(gate))


def _ffn_kernel(x_ref, mod_ref, og_ref, gg_ref, or_ref, rg_ref, osb_ref, gng_ref, rng_ref, bd_ref,
                wout_ref, n2g_ref, wg_ref, wu_ref, wd_ref, fg_ref, o_ref, p_ref, *, final):
    d = D_MODEL
    x = x_ref[0]
    mod = mod_ref[0]
    g1, sh2, sc2, g2 = mod[:, 2 * d:3 * d], mod[:, 3 * d:4 * d], mod[:, 4 * d:5 * d], mod[:, 5 * d:6 * d]
    bd = bd_ref[...]
    o_gla = _head_norm_gate(og_ref[0], gng_ref[...], gg_ref[0], bd)
    o_ret = _head_norm_gate(or_ref[0], rng_ref[...], rg_ref[0], bd)
    o = jnp.concatenate([o_gla.astype(BF16), o_ret.astype(BF16), osb_ref[0]], axis=1)
    x1 = x + g1 * _dot(o, wout_ref[...])
    ms = jnp.mean(x1 * x1, axis=-1, keepdims=True)
    h = (x1 * lax.rsqrt(ms + EPS) * (n2g_ref[...] * (1.0 + sc2)) + sh2).astype(BF16)
    for c0 in range(0, D_FF, FF_CHUNK):
        a = _dot(h, wg_ref[:, c0:c0 + FF_CHUNK])
        u = _dot(h, wu_ref[:, c0:c0 + FF_CHUNK])
        p_ref[:, c0:c0 + FF_CHUNK] = (a * _sigmoid(a) * u).astype(BF16)
    x2 = x1 + g2 * _dot(p_ref[...], wd_ref[...])
    if final:
        ms = jnp.mean(x2 * x2, axis=-1, keepdims=True)
        x2 = x2 * lax.rsqrt(ms + EPS) * fg_ref[...]
    o_ref[0] = x2


def _out_ffn(x, mod, og, gg, orr, rg, osb, gng, rng, bd, wout, n2g, wg, wu, wd, fg, layer, final):
    b, s, d = x.shape
    tm = min(TOKEN_TILE, s)
    tok = lambda width: pl.BlockSpec((1, tm, width), lambda bi, i: (bi, i, 0))
    const = lambda a: pl.BlockSpec(a.shape, lambda bi, i: tuple(0 for _ in a.shape),
                                   pipeline_mode=pl.Buffered(1))
    weight = lambda a: pl.BlockSpec((None,) + a.shape[1:], lambda bi, i: (layer, 0, 0),
                                    pipeline_mode=pl.Buffered(1))
    return pl.pallas_call(
        functools.partial(_ffn_kernel, final=final),
        grid=(b, s // tm),
        in_specs=[tok(d), pl.BlockSpec((1, 1, 6 * d), lambda bi, i: (bi, 0, 0)),
                  tok(GLA_V), tok(GLA_V), tok(RET_W), tok(RET_W), tok(SB_W),
                  const(gng), const(rng), const(bd), weight(wout), const(n2g),
                  weight(wg), weight(wu), weight(wd), const(fg)],
        out_specs=tok(d),
        out_shape=jax.ShapeDtypeStruct((b, s, d), F32),
        scratch_shapes=[pltpu.VMEM((tm, D_FF), BF16)],
        compiler_params=_params("arbitrary", "arbitrary"),
        name="out_projection_ffn",
    )(x, mod, og, gg, orr, rg, osb, gng, rng, bd, wout, n2g, wg, wu, wd, fg)


def _rope_tables(s):
    inv = ROPE_BASE ** (-np.arange(0, RET_DK, 2, dtype=np.float64) / RET_DK)
    ang = np.arange(s, dtype=np.float64)[:, None] * inv[None, :]
    cos = np.repeat(np.cos(ang), 2, axis=1)
    sin = np.repeat(np.sin(ang), 2, axis=1) * np.tile([-1.0, 1.0], RET_DK // 2)[None, :]
    reps = LANES // RET_DK
    return jnp.asarray(np.tile(cos, (1, reps)), F32), jnp.asarray(np.tile(sin, (1, reps)), F32)


def kernel(x, c, ada_w, ada_b, norm1_g, norm2_g, w_in, gla_wa2, gla_ba, gla_norm_g, ret_norm_g,
           w_out, ffn_wg, ffn_wu, ffn_wd, final_g):
    b, s, d = x.shape
    mods = _modulation(c, ada_w, ada_b)
    cos, sin = _rope_tables(s)
    gla_tables = _gla_tables()
    ret_tables = _ret_tables()
    hd = np.arange(GLA_V)
    bd = jnp.asarray((hd[:, None] // GLA_DV == hd[None, :] // GLA_DV) / float(GLA_DV), BF16)
    w_in_b = _prep_w_in(w_in)
    w_out_b, wg_b, wu_b, wd_b = (a.astype(BF16) for a in (w_out, ffn_wg, ffn_wu, ffn_wd))
    for l in range(DEPTH):
        wa2 = jnp.concatenate([gla_wa2[l], jnp.zeros((LANES - GLA_RANK, GLA_QK), F32)], axis=0).astype(BF16)
        mod = mods[l].reshape(b, 1, 6 * d)
        o_gla, gg, o_ret, rg, o_sb = _project_and_mix(
            x, mod, norm1_g[l].reshape(1, d), w_in_b, l, wa2, gla_ba[l].reshape(1, GLA_QK), cos, sin,
            gla_tables, ret_tables)
        x = _out_ffn(x, mod, o_gla, gg, o_ret, rg, o_sb,
                     jnp.tile(gla_norm_g[l], GLA_HEADS).reshape(1, GLA_V),
                     jnp.tile(ret_norm_g[l], RET_HEADS).reshape(1, RET_W), bd,
                     w_out_b, norm2_g[l].reshape(1, d), wg_b, wu_b, wd_b,
                     final_g.reshape(1, d), l, l == DEPTH - 1)
    return x
```

```python
import functools

import jax
import jax.numpy as jnp
import numpy as np
from jax import lax
from jax.experimental import pallas as pl
from jax.experimental.pallas import tpu as pltpu

F32 = jnp.float32
BF16 = jnp.bfloat16

D_MODEL = 1024
DEPTH = 2
GLA_HEADS, GLA_DK, GLA_DV, GLA_RANK = 4, 32, 64, 16
GLA_GATE_NORM = 16.0
RET_HEADS, RET_DK, RET_DV = 4, 64, 64
ROPE_BASE = 10000.0
SB_HEADS, SB_DK, SB_DV = 8, 64, 64
D_FF = 2816
EPS = 1e-6

GLA_QK = GLA_HEADS * GLA_DK
GLA_V = GLA_HEADS * GLA_DV
RET_W = RET_HEADS * RET_DK
SB_W = SB_HEADS * SB_DK
C_GQ, C_GK, C_GV, C_GG = 0, 128, 256, 512
C_RQ, C_RK, C_RV, C_RG = 768, 1024, 1280, 1536
C_SQ, C_SK, C_SV = 1792, 2304, 2816
C_GR = 3328
IN_COLS_PADDED = 3456

LANES = 128
GLA_CHUNK = 64
GLA_LEVELS = GLA_CHUNK.bit_length() - 1
RET_CHUNK = 256
SB_TQ = 512
SB_TK = 256
SB_UNDERFLOW_LOG2 = 160.0
SB_EXP_CLAMP = 64.0
TOKEN_TILE = 512
FF_CHUNK = 256
MOD_COLS = 1536
VMEM_LIMIT = 56 * 1024 * 1024


def _dot(a, b):
    return jnp.dot(a, b, preferred_element_type=F32)


def _dot_nt(a, b):
    return lax.dot_general(a, b, (((1,), (1,)), ((), ())), preferred_element_type=F32)


def _dot_tn(a, b):
    return lax.dot_general(a, b, (((0,), (0,)), ((), ())), preferred_element_type=F32)


def _sigmoid(x):
    return 1.0 / (1.0 + jnp.exp(-x))


def _softplus(x):
    return jnp.maximum(x, 0.0) + jnp.log(1.0 + jnp.exp(-jnp.abs(x)))


def _log2e(width):
    return 1.0 / jnp.log(jnp.full((1, width), 2.0, F32))


def _params(*sem):
    return pltpu.CompilerParams(dimension_semantics=sem, vmem_limit_bytes=VMEM_LIMIT)


def _mod_kernel(c_ref, w_ref, b_ref, o_ref):
    c = c_ref[...]
    cs = c * _sigmoid(c)
    o_ref[0] = jnp.dot(cs, w_ref[0], preferred_element_type=F32,
                       precision=lax.Precision.HIGHEST) + b_ref[0]


def _modulation(c, ada_w, ada_b):
    depth, d, n = ada_w.shape
    b = c.shape[0]
    tn = MOD_COLS
    return pl.pallas_call(
        _mod_kernel,
        grid=(depth, n // tn),
        in_specs=[pl.BlockSpec((b, d), lambda l, j: (0, 0)),
                  pl.BlockSpec((1, d, tn), lambda l, j: (l, 0, j)),
                  pl.BlockSpec((1, 1, tn), lambda l, j: (l, 0, j))],
        out_specs=pl.BlockSpec((1, b, tn), lambda l, j: (l, 0, j)),
        out_shape=jax.ShapeDtypeStruct((depth, b, n), F32),
        compiler_params=_params("arbitrary", "arbitrary"),
        name="adaln_modulation",
    )(c, ada_w, ada_b.reshape(depth, 1, n))


def _rope(t, cos, sin):
    lane = lax.broadcasted_iota(jnp.int32, cos.shape, 1)
    even = (lane & 1) == 0
    outs = []
    for c in range(t.shape[1] // LANES):
        tc = t[:, c * LANES:(c + 1) * LANES]
        nxt = pltpu.roll(tc, LANES - 1, 1)
        prv = pltpu.roll(tc, 1, 1)
        outs.append(tc * cos + jnp.where(even, nxt, prv) * sin)
    return jnp.concatenate(outs, axis=1)


def _proj_mix_kernel(x_ref, mod_ref, g_ref, w_ref, wa2_ref, ba_ref, cos_ref, sin_ref,
                     mcat_ref, lmask_ref, kmask_ref, vmask_ref, gsmask_ref,
                     dall_ref, xi_ref, zeta_ref, gc_ref, hmask_ref, rsmask_ref, tri_ref,
                     og_ref, gg_ref, or_ref, rg_ref, osb_ref,
                     gst_ref, rst_ref, skT_ref, sv_ref, acc_ref, c_ref):
    d = D_MODEL
    i = pl.program_id(1)

    @pl.when(pl.program_id(1) == 0)
    def _():
        gst_ref[...] = jnp.zeros_like(gst_ref)
        rst_ref[...] = jnp.zeros_like(rst_ref)

    x = x_ref[0]
    mod = mod_ref[0]
    sh1, sc1 = mod[:, 0:d], mod[:, d:2 * d]
    ms = jnp.mean(x * x, axis=-1, keepdims=True)
    h = (x * lax.rsqrt(ms + EPS) * (g_ref[...] * (1.0 + sc1)) + sh1).astype(BF16)

    def proj(a, width):
        return _dot(h, w_ref[:, a:a + width])

    sq = (proj(C_SQ, SB_W) * (_log2e(SB_W) * SB_DK ** -0.5)).astype(BF16)
    skT = proj(C_SK, SB_W).T.astype(BF16)
    sv_head = proj(C_SV, SB_W - LANES)
    sv_tail_gr = proj(C_GR - LANES, 2 * LANES)
    sv = jnp.concatenate([sv_head, sv_tail_gr[:, :LANES]], axis=1).astype(BF16)
    tokens = pl.ds(pl.multiple_of(i * SB_TQ, SB_TQ), SB_TQ)
    skT_ref[:, tokens] = skT
    sv_ref[tokens, :] = sv

    vals = {}
    tm = x.shape[0]
    hm = tm // 2
    assert hm == RET_CHUNK

    def proj_half(name, col, width, part, post=lambda t, rows: t):
        def run():
            rows = slice(part * hm, (part + 1) * hm)
            vals[name, part] = post(_dot(h[rows], w_ref[:, col:col + width]), rows)
        return run

    def rope_q(t, rows):
        return _rope(t, cos_ref[rows, :], sin_ref[rows, :])

    def rope_k(t, rows):
        return _rope(t, cos_ref[rows, :], sin_ref[rows, :]) * (RET_DK ** -0.5)

    def gla(part):
        def run():
            rows = slice(part * hm, (part + 1) * hm)
            gqk = vals["gqk", part]
            u = _dot(sv_tail_gr[rows, LANES:].astype(BF16), wa2_ref[...]) + ba_ref[...]
            la = -_softplus(-u) * (1.0 / GLA_GATE_NORM)
            gg_ref[0, rows, :] = vals["gg", part]
            og_ref[0, rows, :] = _gla_block(gqk[:, :GLA_QK] * (GLA_DK ** -0.5), gqk[:, GLA_QK:], vals["gv", part], la,
                                            mcat_ref, lmask_ref, kmask_ref, vmask_ref, gsmask_ref, gst_ref)
        return run

    def retention(part):
        def run():
            rows = slice(part * hm, (part + 1) * hm)
            rg_ref[0, rows, :] = vals["rg", part]
            or_ref[0, rows, :] = _ret_chunk(vals["rq", part], vals["rk", part], vals["rv", part], dall_ref,
                                            xi_ref, zeta_ref, gc_ref, hmask_ref, rsmask_ref, rst_ref)
        return run

    pieces = []
    for part in range(2):
        pieces += [proj_half("gqk", C_GQ, 2 * GLA_QK, part), proj_half("gv", C_GV, GLA_V, part),
                   proj_half("gg", C_GG, GLA_V, part), gla(part),
                   proj_half("rq", C_RQ, RET_W, part, rope_q), proj_half("rk", C_RK, RET_W, part, rope_k),
                   proj_half("rv", C_RV, RET_W, part), proj_half("rg", C_RG, RET_W, part), retention(part)]
    osb_ref[0] = _sb_block(i, sq, skT, sv, skT_ref, sv_ref, tri_ref, acc_ref, c_ref, interleave=pieces)


def _prep_w_in_kernel(w_ref, o_ref):
    w = w_ref[0]
    gr0 = 2 * GLA_QK + 2 * GLA_V
    o_ref[0] = jnp.concatenate(
        [w[:, :gr0], w[:, gr0 + GLA_RANK:], w[:, gr0:gr0 + GLA_RANK],
         jnp.zeros((w.shape[0], IN_COLS_PADDED - w.shape[1]), w.dtype)], axis=1).astype(BF16)


def _prep_w_in(w_in):
    depth, d, n = w_in.shape
    tr = LANES
    return pl.pallas_call(
        _prep_w_in_kernel,
        grid=(depth, d // tr),
        in_specs=[pl.BlockSpec((1, tr, n), lambda l, i: (l, i, 0))],
        out_specs=pl.BlockSpec((1, tr, IN_COLS_PADDED), lambda l, i: (l, i, 0)),
        out_shape=jax.ShapeDtypeStruct((depth, d, IN_COLS_PADDED), BF16),
        compiler_params=_params("arbitrary", "arbitrary"),
        name="w_in_prep",
    )(w_in)


def _project_and_mix(x, mod, g, w_all, layer, wa2, ba, cos, sin, gla_tables, ret_tables):
    b, s, d = x.shape
    tm = TOKEN_TILE
    assert tm == SB_TQ and s % tm == 0 and tm % GLA_CHUNK == 0 and tm % RET_CHUNK == 0
    tok = lambda width: pl.BlockSpec((1, tm, width), lambda bi, i: (bi, i, 0))
    const = lambda a: pl.BlockSpec(a.shape, lambda bi, i: tuple(0 for _ in a.shape))
    f32 = lambda width: jax.ShapeDtypeStruct((b, s, width), F32)
    r = np.arange(SB_TK)
    tri = jnp.asarray(r[:, None] > r[None, :], BF16)
    consts = (wa2, ba)
    return pl.pallas_call(
        _proj_mix_kernel,
        grid=(b, s // tm),
        in_specs=[tok(d), pl.BlockSpec((1, 1, 6 * d), lambda bi, i: (bi, 0, 0)), const(g),
                  pl.BlockSpec((None,) + w_all.shape[1:], lambda bi, i: (layer, 0, 0))]
                 + [const(a) for a in consts]
                 + [pl.BlockSpec((tm, LANES), lambda bi, i: (i, 0))] * 2
                 + [const(a) for a in gla_tables + ret_tables + (tri,)],
        out_specs=[tok(GLA_V), tok(GLA_V), tok(RET_W), tok(RET_W), tok(SB_W)],
        out_shape=[f32(GLA_V), f32(GLA_V), f32(RET_W), f32(RET_W),
                   jax.ShapeDtypeStruct((b, s, SB_W), BF16)],
        scratch_shapes=[pltpu.VMEM((GLA_V, GLA_QK), F32), pltpu.VMEM((RET_W, RET_W), F32),
                        pltpu.VMEM((SB_W, s), BF16), pltpu.VMEM((s, SB_W), BF16),
                        pltpu.VMEM((SB_HEADS, tm, LANES), F32), pltpu.VMEM((SB_HEADS, tm, LANES), F32)],
        compiler_params=_params("arbitrary", "arbitrary"),
        name="project_and_mix",
    )(x, mod, g, w_all, *consts, cos, sin, *gla_tables, *ret_tables, tri)


def _gla_tables():
    c, nl = GLA_CHUNK, GLA_LEVELS
    t = np.arange(c)
    rows = []
    for l in range(nl):
        w = 1 << l
        start = (t // w) * w
        rows.append((t[None, :] >= start[:, None]) & (t[None, :] <= t[:, None]))
    for l in range(nl):
        w = 1 << l
        end = (t // w + 1) * w - 1
        rows.append((t[None, :] > t[:, None]) & (t[None, :] <= end[:, None]))
    rows.append(t[None, :] <= t[:, None])
    mcat = np.concatenate(rows, 0).astype(np.float32)
    mcat3 = np.concatenate([mcat] * 3, 1)
    masks = []
    for l in range(nl):
        w = 1 << l
        same = (t[:, None] // (2 * w)) == (t[None, :] // (2 * w))
        upper = ((t // w) % 2 == 1)[:, None]
        lower = ((t // w) % 2 == 0)[None, :]
        masks.append(np.tile(same & upper & lower, (1, GLA_HEADS)))
    masks.append(np.tile(np.eye(c, dtype=bool), (1, GLA_HEADS)))
    lmask = np.stack(masks).astype(np.float32)
    r = np.arange(GLA_HEADS * c)
    kmask = (r[:, None] // c == np.arange(GLA_QK)[None, :] // GLA_DK).astype(np.float32)
    vmask = (r[:, None] // c == np.arange(GLA_V)[None, :] // GLA_DV).astype(np.float32)
    smask = (np.arange(GLA_V)[:, None] // GLA_DV == np.arange(GLA_QK)[None, :] // GLA_DK).astype(np.float32)
    return (jnp.asarray(mcat3, BF16), jnp.asarray(lmask), jnp.asarray(kmask),
            jnp.asarray(vmask), jnp.asarray(smask))


def _split3(x):
    hi = x.astype(BF16)
    r1 = x - hi.astype(F32)
    mid = r1.astype(BF16)
    lo = (r1 - mid.astype(F32)).astype(BF16)
    return hi, mid, lo


def _gla_block(q_all, k_all, v_all, la_all, mcat_ref, lmask_ref, kmask_ref, vmask_ref, smask_ref, st_ref):
    c, nl = GLA_CHUNK, GLA_LEVELS
    nchunk = q_all.shape[0] // c
    kmask, vmask, smask = kmask_ref[...], vmask_ref[...], smask_ref[...]
    la3 = jnp.concatenate([jnp.concatenate(_split3(la_all[ci * c:(ci + 1) * c]), axis=0)
                           for ci in range(nchunk)], axis=1)
    cs_all = _dot(mcat_ref[...], la3)
    e_all = jnp.exp(cs_all)
    chunks = range(nchunk)
    q = [q_all[ci * c:(ci + 1) * c] for ci in chunks]
    k = [k_all[ci * c:(ci + 1) * c] for ci in chunks]
    v = [v_all[ci * c:(ci + 1) * c] for ci in chunks]
    cs = [cs_all[:, ci * GLA_QK:(ci + 1) * GLA_QK] for ci in chunks]
    e = [e_all[:, ci * GLA_QK:(ci + 1) * GLA_QK] for ci in chunks]
    scores = [jnp.zeros((c, GLA_HEADS * c), F32) for _ in chunks]
    for l in range(nl + 1):
        for ci in chunks:
            if l < nl:
                qt = q[ci] * e[ci][l * c:(l + 1) * c]
                kt = k[ci] * e[ci][(nl + l) * c:(nl + l + 1) * c]
            else:
                qt, kt = q[ci], k[ci]
            krows = (jnp.concatenate([kt] * GLA_HEADS, axis=0) * kmask).astype(BF16)
            scores[ci] = scores[ci] + _dot_nt(qt.astype(BF16), krows) * lmask_ref[l]
    intra = [_dot(scores[ci].astype(BF16), (jnp.concatenate([v[ci]] * GLA_HEADS, axis=0) * vmask).astype(BF16))
             for ci in chunks]
    bcum = [cs[ci][2 * nl * c:(2 * nl + 1) * c] for ci in chunks]
    qd = [(q[ci] * e[ci][2 * nl * c:(2 * nl + 1) * c]).astype(BF16) for ci in chunks]
    upd = [_dot_tn(v[ci].astype(BF16), (k[ci] * jnp.exp(bcum[ci][c - 1:c, :] - bcum[ci])).astype(BF16)) * smask
           for ci in chunks]
    outs = []
    st = st_ref[...]
    for ci in chunks:
        outs.append(intra[ci] + _dot_nt(qd[ci], st.astype(BF16)))
        st = st * jnp.exp(bcum[ci][c - 1:c, :]) + upd[ci]
    st_ref[...] = st
    return jnp.concatenate(outs, axis=0)


def _ret_tables():
    c = RET_CHUNK
    gam = 1.0 - 2.0 ** (-5.0 - np.arange(RET_HEADS, dtype=np.float64))
    t = np.arange(c)
    rel = t[:, None] - t[None, :]
    dmat = np.where(rel[None] >= 0, gam[:, None, None] ** np.maximum(rel, 0)[None], 0.0)
    dall = np.concatenate(list(dmat), axis=1)
    xi = np.repeat((gam[None, :] ** (t[:, None] + 1.0)), RET_DK, axis=1)
    zeta = np.repeat((gam[None, :] ** (c - 1.0 - t[:, None])), RET_DK, axis=1)
    gc = np.repeat(gam ** c, RET_DK)[:, None] * np.ones((1, RET_W))
    r = np.arange(RET_HEADS * c)
    hmask = (r[:, None] // c == np.arange(RET_W)[None, :] // RET_DK)
    smask = (np.arange(RET_W)[:, None] // RET_DK == np.arange(RET_W)[None, :] // RET_DV)
    return tuple(jnp.asarray(a, F32) for a in (dall, xi, zeta, gc, hmask, smask))


def _ret_chunk(q, k, v, dall_ref, xi_ref, zeta_ref, gc_ref, hmask_ref, smask_ref, r_ref):
    hmask = hmask_ref[...]
    krows = (jnp.concatenate([k] * RET_HEADS, axis=0) * hmask).astype(BF16)
    scores = _dot_nt(q.astype(BF16), krows) * dall_ref[...]
    vbd = (jnp.concatenate([v] * RET_HEADS, axis=0) * hmask).astype(BF16)
    o = _dot(scores.astype(BF16), vbd)
    r = r_ref[...]
    o = o + _dot((q * xi_ref[...]).astype(BF16), r.astype(BF16))
    upd = _dot_tn((k * zeta_ref[...]).astype(BF16), v.astype(BF16))
    r_ref[...] = r * gc_ref[...] + upd * smask_ref[...]
    return o


def _sb_block(i, q, kT_new, v_new, kT_ref, v_ref, w_ref, acc_ref, c_ref, interleave):
    tq, tk = SB_TQ, SB_TK
    npairs = SB_HEADS // 2
    lane = lax.broadcasted_iota(jnp.int32, (tq, LANES), 1)
    qs = []
    for pr in range(npairs):
        q2 = q[:, pr * LANES:(pr + 1) * LANES]
        zero = jnp.zeros_like(q2)
        qs += [jnp.where(lane < SB_DK, q2, zero), jnp.where(lane >= SB_DK, q2, zero)]
    w = w_ref[...]

    def softplus2(t):
        return jnp.maximum(jnp.log(1.0 + jnp.exp2(jnp.minimum(t, SB_EXP_CLAMP))) * _log2e(t.shape[1]), t)

    def masked_softplus(zs):
        hq = tk // 2
        below = (lax.broadcasted_iota(jnp.int32, (hq, hq), 1) < lax.broadcasted_iota(jnp.int32, (hq, hq), 0))
        z_ul, z_ll, z_lr = zs[:hq, :hq], zs[hq:, :hq], zs[hq:, hq:]
        sp_ul = jnp.where(below, softplus2(z_ul), 0.0)
        sp_ll = softplus2(z_ll)
        sp_lr = jnp.where(below, softplus2(z_lr), 0.0)
        hidden = jnp.zeros((hq, hq), F32)
        sp = jnp.concatenate([jnp.concatenate([sp_ul, hidden], axis=1),
                              jnp.concatenate([sp_ll, sp_lr], axis=1)], axis=0)

        def p_of(t):
            p_ul = jnp.where(below, jnp.exp2((z_ul - sp_ul) - t[:hq, :hq]), 0.0)
            p_ll = jnp.exp2((z_ll - sp_ll) - t[hq:, :hq])
            p_lr = jnp.where(below, jnp.exp2((z_lr - sp_lr) - t[hq:, hq:]), 0.0)
            return jnp.concatenate([jnp.concatenate([p_ul, hidden], axis=1),
                                    jnp.concatenate([p_ll, p_lr], axis=1)], axis=0)

        return sp, p_of

    def tile(j, r0, r1, mask_rows, local=None, valid=None, heads=range(SB_HEADS)):
        assert mask_rows in (0, tk)
        start = pl.multiple_of(j * tk, tk)
        sections = [(r0, r0 + mask_rows, True), (r0 + mask_rows, r1, False)]
        for pr in sorted({hd // 2 for hd in heads}):
            cols = slice(pr * LANES, (pr + 1) * LANES)
            if local is None:
                kT2 = kT_ref[cols, pl.ds(start, tk)]
                v2 = v_ref[pl.ds(start, tk), cols]
            else:
                kT2 = kT_new[cols, local * tk:(local + 1) * tk]
                v2 = v_new[local * tk:(local + 1) * tk, cols]
            if valid is not None:
                v2 = jnp.where(valid, v2, jnp.zeros_like(v2))
            for hd in (h for h in (2 * pr, 2 * pr + 1) if h in heads):
                z = _dot(qs[hd][r0:r1], kT2)
                for a, b, masked in sections:
                    if a == b:
                        continue
                    zs = z[a - r0:b - r0]
                    if masked:
                        sp, p_of = masked_softplus(zs)
                        r = _dot(sp.astype(BF16), w)
                        c_ref[hd, a:b, :] = jnp.broadcast_to(r[:, 0:1] + sp[:, 0:1], (b - a, LANES))
                        acc_ref[hd, a:b, :] = _dot(p_of(r).astype(BF16), v2)
                        continue
                    c = c_ref[hd, a:b, :]
                    sp = softplus2(zs)
                    r = _dot(sp.astype(BF16), w)
                    p = jnp.exp2((zs - sp) - (r + jnp.concatenate([c] * (tk // LANES), axis=1)))
                    total = r[:, 0:1] + sp[:, 0:1]
                    if valid is not None:
                        total = jnp.where(valid, total, 0.0)
                    c_ref[hd, a:b, :] = c + jnp.broadcast_to(total, c.shape)
                    acc_ref[hd, a:b, :] += _dot(p.astype(BF16), v2)

    nk = tq // tk
    half = tq // 2
    fillers = list(interleave)
    for hd in range(SB_HEADS):
        for d in reversed(range(nk)):
            tile(i * nk + d, d * tk, tq, tk, local=d, heads=(hd,))
            if fillers:
                fillers.pop(0)()
        tile(jnp.maximum(i * nk - 1, 0), 0, half, 0, valid=i > 0, heads=(hd,))
        if fillers:
            fillers.pop(0)()
    assert not fillers

    def mins():
        c = c_ref[...]
        return jnp.min(c[:, :half]), jnp.min(c[:, half:])

    def cond(st):
        return jnp.logical_and(st[0] < i * nk, jnp.minimum(st[1], st[2]) < SB_UNDERFLOW_LOG2)

    def body(st):
        j = i * nk - 1 - st[0]
        lower_done = st[2] >= SB_UNDERFLOW_LOG2

        def first():
            @pl.when(jnp.logical_not(lower_done))
            def _():
                tile(j, half, tq, 0)

        def later():
            lax.cond(lower_done, lambda: tile(j, 0, half, 0), lambda: tile(j, 0, tq, 0))

        lax.cond(st[0] == 0, first, later)
        return (st[0] + 1,) + mins()

    lax.while_loop(cond, body, (jnp.int32(0),) + mins())
    return jnp.concatenate(
        [jnp.where(lane < SB_DK, acc_ref[2 * pr], acc_ref[2 * pr + 1]).astype(BF16) for pr in range(npairs)], axis=1)


def _head_norm_gate(o, g, gate, bd):
    ms = _dot((o * o).astype(BF16), bd)
    return o * lax.rsqrt(ms + EPS) * g * (gate * _sigmoid(gate))


def _ffn_kernel(x_ref, mod_ref, og_ref, gg_ref, or_ref, rg_ref, osb_ref, gng_ref, rng_ref, bd_ref,
                wout_ref, n2g_ref, wg_ref, wu_ref, wd_ref, fg_ref, o_ref, p_ref, *, final):
    d = D_MODEL
    x = x_ref[0]
    mod = mod_ref[0]
    g1, sh2, sc2, g2 = mod[:, 2 * d:3 * d], mod[:, 3 * d:4 * d], mod[:, 4 * d:5 * d], mod[:, 5 * d:6 * d]
    bd = bd_ref[...]
    o_gla = _head_norm_gate(og_ref[0], gng_ref[...], gg_ref[0], bd)
    o_ret = _head_norm_gate(or_ref[0], rng_ref[...], rg_ref[0], bd)
    o = jnp.concatenate([o_gla.astype(BF16), o_ret.astype(BF16), osb_ref[0]], axis=1)
    x1 = x + g1 * _dot(o, wout_ref[...])
    ms = jnp.mean(x1 * x1, axis=-1, keepdims=True)
    h = (x1 * lax.rsqrt(ms + EPS) * (n2g_ref[...] * (1.0 + sc2)) + sh2).astype(BF16)
    for c0 in range(0, D_FF, FF_CHUNK):
        a = _dot(h, wg_ref[:, c0:c0 + FF_CHUNK])
        u = _dot(h, wu_ref[:, c0:c0 + FF_CHUNK])
        p_ref[:, c0:c0 + FF_CHUNK] = (a * _sigmoid(a) * u).astype(BF16)
    x2 = x1 + g2 * _dot(p_ref[...], wd_ref[...])
    if final:
        ms = jnp.mean(x2 * x2, axis=-1, keepdims=True)
        x2 = x2 * lax.rsqrt(ms + EPS) * fg_ref[...]
    o_ref[0] = x2


def _out_ffn(x, mod, og, gg, orr, rg, osb, gng, rng, bd, wout, n2g, wg, wu, wd, fg, layer, final):
    b, s, d = x.shape
    tm = min(TOKEN_TILE, s)
    tok = lambda width: pl.BlockSpec((1, tm, width), lambda bi, i: (bi, i, 0))
    const = lambda a: pl.BlockSpec(a.shape, lambda bi, i: tuple(0 for _ in a.shape),
                                   pipeline_mode=pl.Buffered(1))
    weight = lambda a: pl.BlockSpec((None,) + a.shape[1:], lambda bi, i: (layer, 0, 0),
                                    pipeline_mode=pl.Buffered(1))
    return pl.pallas_call(
        functools.partial(_ffn_kernel, final=final),
        grid=(b, s // tm),
        in_specs=[tok(d), pl.BlockSpec((1, 1, 6 * d), lambda bi, i: (bi, 0, 0)),
                  tok(GLA_V), tok(GLA_V), tok(RET_W), tok(RET_W), tok(SB_W),
                  const(gng), const(rng), const(bd), weight(wout), const(n2g),
                  weight(wg), weight(wu), weight(wd), const(fg)],
        out_specs=tok(d),
        out_shape=jax.ShapeDtypeStruct((b, s, d), F32),
        scratch_shapes=[pltpu.VMEM((tm, D_FF), BF16)],
        compiler_params=_params("arbitrary", "arbitrary"),
        name="out_projection_ffn",
    )(x, mod, og, gg, orr, rg, osb, gng, rng, bd, wout, n2g, wg, wu, wd, fg)


def _rope_tables(s):
    inv = ROPE_BASE ** (-np.arange(0, RET_DK, 2, dtype=np.float64) / RET_DK)
    ang = np.arange(s, dtype=np.float64)[:, None] * inv[None, :]
    cos = np.repeat(np.cos(ang), 2, axis=1)
    sin = np.repeat(np.sin(ang), 2, axis=1) * np.tile([-1.0, 1.0], RET_DK // 2)[None, :]
    reps = LANES // RET_DK
    return jnp.asarray(np.tile(cos, (1, reps)), F32), jnp.asarray(np.tile(sin, (1, reps)), F32)


def kernel(x, c, ada_w, ada_b, norm1_g, norm2_g, w_in, gla_wa2, gla_ba, gla_norm_g, ret_norm_g,
           w_out, ffn_wg, ffn_wu, ffn_wd, final_g):
    b, s, d = x.shape
    mods = _modulation(c, ada_w, ada_b)
    cos, sin = _rope_tables(s)
    gla_tables = _gla_tables()
    ret_tables = _ret_tables()
    hd = np.arange(GLA_V)
    bd = jnp.asarray((hd[:, None] // GLA_DV == hd[None, :] // GLA_DV) / float(GLA_DV), BF16)
    w_in_b = _prep_w_in(w_in)
    w_out_b, wg_b, wu_b, wd_b = (a.astype(BF16) for a in (w_out, ffn_wg, ffn_wu, ffn_wd))
    for l in range(DEPTH):
        wa2 = jnp.concatenate([gla_wa2[l], jnp.zeros((LANES - GLA_RANK, GLA_QK), F32)], axis=0).astype(BF16)
        mod = mods[l].reshape(b, 1, 6 * d)
        o_gla, gg, o_ret, rg, o_sb = _project_and_mix(
            x, mod, norm1_g[l].reshape(1, d), w_in_b, l, wa2, gla_ba[l].reshape(1, GLA_QK), cos, sin,
            gla_tables, ret_tables)
        x = _out_ffn(x, mod, o_gla, gg, o_ret, rg, o_sb,
                     jnp.tile(gla_norm_g[l], GLA_HEADS).reshape(1, GLA_V),
                     jnp.tile(ret_norm_g[l], RET_HEADS).reshape(1, RET_W), bd,
                     w_out_b, norm2_g[l].reshape(1, d), wg_b, wu_b, wd_b,
                     final_g.reshape(1, d), l, l == DEPTH - 1)
    return x
```

```python
import functools

import jax
import jax.numpy as jnp
import numpy as np
from jax import lax
from jax.experimental import pallas as pl
from jax.experimental.pallas import tpu as pltpu

F32 = jnp.float32
BF16 = jnp.bfloat16

D_MODEL = 1024
DEPTH = 2
GLA_HEADS, GLA_DK, GLA_DV, GLA_RANK = 4, 32, 64, 16
GLA_GATE_NORM = 16.0
RET_HEADS, RET_DK, RET_DV = 4, 64, 64
ROPE_BASE = 10000.0
SB_HEADS, SB_DK, SB_DV = 8, 64, 64
D_FF = 2816
EPS = 1e-6

GLA_QK = GLA_HEADS * GLA_DK
GLA_V = GLA_HEADS * GLA_DV
RET_W = RET_HEADS * RET_DK
SB_W = SB_HEADS * SB_DK
C_GQ, C_GK, C_GV, C_GG = 0, 128, 256, 512
C_RQ, C_RK, C_RV, C_RG = 768, 1024, 1280, 1536
C_SQ, C_SK, C_SV = 1792, 2304, 2816
C_GR = 3328
IN_COLS_PADDED = 3456

LANES = 128
GLA_CHUNK = 64
GLA_LEVELS = GLA_CHUNK.bit_length() - 1
RET_CHUNK = 256
SB_TQ = 512
SB_TK = 256
SB_UNDERFLOW_LOG2 = 160.0
SB_EXP_CLAMP = 64.0
TOKEN_TILE = 512
FF_CHUNK = 256
MOD_COLS = 1536
VMEM_LIMIT = 56 * 1024 * 1024


def _dot(a, b):
    return jnp.dot(a, b, preferred_element_type=F32)


def _dot_nt(a, b):
    return lax.dot_general(a, b, (((1,), (1,)), ((), ())), preferred_element_type=F32)


def _dot_tn(a, b):
    return lax.dot_general(a, b, (((0,), (0,)), ((), ())), preferred_element_type=F32)


def _sigmoid(x):
    return 1.0 / (1.0 + jnp.exp(-x))


def _softplus(x):
    return jnp.maximum(x, 0.0) + jnp.log(1.0 + jnp.exp(-jnp.abs(x)))


def _log2e(width):
    return 1.0 / jnp.log(jnp.full((1, width), 2.0, F32))


def _params(*sem):
    return pltpu.CompilerParams(dimension_semantics=sem, vmem_limit_bytes=VMEM_LIMIT)


def _mod_kernel(c_ref, w_ref, b_ref, o_ref):
    c = c_ref[...]
    cs = c * _sigmoid(c)
    n = cs.shape[0]
    cs_hi = cs.astype(BF16)
    cs_lo = (cs - cs_hi.astype(F32)).astype(BF16)
    w = w_ref[0]
    w_hi = w.astype(BF16)
    w_lo = (w - w_hi.astype(F32)).astype(BF16)
    both = _dot(jnp.concatenate([cs_hi, cs_lo], axis=0), w_hi)
    o_ref[0] = both[:n] + both[n:] + _dot(cs_hi, w_lo) + b_ref[0]


def _modulation(c, ada_w, ada_b):
    depth, d, n = ada_w.shape
    b = c.shape[0]
    tn = MOD_COLS
    return pl.pallas_call(
        _mod_kernel,
        grid=(depth, n // tn),
        in_specs=[pl.BlockSpec((b, d), lambda l, j: (0, 0)),
                  pl.BlockSpec((1, d, tn), lambda l, j: (l, 0, j)),
                  pl.BlockSpec((1, 1, tn), lambda l, j: (l, 0, j))],
        out_specs=pl.BlockSpec((1, b, tn), lambda l, j: (l, 0, j)),
        out_shape=jax.ShapeDtypeStruct((depth, b, n), F32),
        compiler_params=_params("arbitrary", "arbitrary"),
        name="adaln_modulation",
    )(c, ada_w, ada_b.reshape(depth, 1, n))


def _rope(t, cos, sin):
    lane = lax.broadcasted_iota(jnp.int32, cos.shape, 1)
    even = (lane & 1) == 0
    outs = []
    for c in range(t.shape[1] // LANES):
        tc = t[:, c * LANES:(c + 1) * LANES]
        nxt = pltpu.roll(tc, LANES - 1, 1)
        prv = pltpu.roll(tc, 1, 1)
        outs.append(tc * cos + jnp.where(even, nxt, prv) * sin)
    return jnp.concatenate(outs, axis=1)


def _proj_mix_kernel(x_ref, mod_ref, g_ref, w_ref, wa2_ref, ba_ref, cos_ref, sin_ref,
                     mcat_ref, lmask_ref, kmask_ref, vmask_ref, gsmask_ref,
                     dall_ref, xi_ref, zeta_ref, gc_ref, hmask_ref, rsmask_ref, tri_ref,
                     og_ref, gg_ref, or_ref, rg_ref, osb_ref,
                     gst_ref, rst_ref, skT_ref, sv_ref, acc_ref, c_ref):
    d = D_MODEL
    i = pl.program_id(1)

    @pl.when(pl.program_id(1) == 0)
    def _():
        gst_ref[...] = jnp.zeros_like(gst_ref)
        rst_ref[...] = jnp.zeros_like(rst_ref)

    x = x_ref[0]
    mod = mod_ref[0]
    sh1, sc1 = mod[:, 0:d], mod[:, d:2 * d]
    ms = jnp.mean(x * x, axis=-1, keepdims=True)
    h = (x * lax.rsqrt(ms + EPS) * (g_ref[...] * (1.0 + sc1)) + sh1).astype(BF16)

    def proj(a, width):
        return _dot(h, w_ref[:, a:a + width])

    sq = (proj(C_SQ, SB_W) * (_log2e(SB_W) * SB_DK ** -0.5)).astype(BF16)
    skT = proj(C_SK, SB_W).T.astype(BF16)
    sv_head = proj(C_SV, SB_W - LANES)
    sv_tail_gr = proj(C_GR - LANES, 2 * LANES)
    sv = jnp.concatenate([sv_head, sv_tail_gr[:, :LANES]], axis=1).astype(BF16)
    tokens = pl.ds(pl.multiple_of(i * SB_TQ, SB_TQ), SB_TQ)
    skT_ref[:, tokens] = skT
    sv_ref[tokens, :] = sv

    vals = {}
    tm = x.shape[0]
    hm = tm // 2
    assert hm == RET_CHUNK

    def proj_half(name, col, width, part, post=lambda t, rows: t):
        def run():
            rows = slice(part * hm, (part + 1) * hm)
            vals[name, part] = post(_dot(h[rows], w_ref[:, col:col + width]), rows)
        return run

    def rope_q(t, rows):
        return _rope(t, cos_ref[rows, :], sin_ref[rows, :])

    def rope_k(t, rows):
        return _rope(t, cos_ref[rows, :], sin_ref[rows, :]) * (RET_DK ** -0.5)

    def gla(part):
        def run():
            rows = slice(part * hm, (part + 1) * hm)
            gqk = vals["gqk", part]
            u = _dot(sv_tail_gr[rows, LANES:].astype(BF16), wa2_ref[...]) + ba_ref[...]
            la = -_softplus(-u) * (1.0 / GLA_GATE_NORM)
            gg_ref[0, rows, :] = vals["gg", part]
            og_ref[0, rows, :] = _gla_block(gqk[:, :GLA_QK] * (GLA_DK ** -0.5), gqk[:, GLA_QK:], vals["gv", part], la,
                                            mcat_ref, lmask_ref, kmask_ref, vmask_ref, gsmask_ref, gst_ref)
        return run

    def retention(part):
        def run():
            rows = slice(part * hm, (part + 1) * hm)
            rg_ref[0, rows, :] = vals["rg", part]
            or_ref[0, rows, :] = _ret_chunk(vals["rq", part], vals["rk", part], vals["rv", part], dall_ref,
                                            xi_ref, zeta_ref, gc_ref, hmask_ref, rsmask_ref, rst_ref)
        return run

    pieces = []
    for part in range(2):
        pieces += [proj_half("gqk", C_GQ, 2 * GLA_QK, part), proj_half("gv", C_GV, GLA_V, part),
                   proj_half("gg", C_GG, GLA_V, part), gla(part),
                   proj_half("rq", C_RQ, RET_W, part, rope_q), proj_half("rk", C_RK, RET_W, part, rope_k),
                   proj_half("rv", C_RV, RET_W, part), proj_half("rg", C_RG, RET_W, part), retention(part)]
    osb_ref[0] = _sb_block(i, sq, skT, sv, skT_ref, sv_ref, tri_ref, acc_ref, c_ref, interleave=pieces)


def _prep_w_in_kernel(w_ref, o_ref):
    w = w_ref[0]
    gr0 = 2 * GLA_QK + 2 * GLA_V
    o_ref[0] = jnp.concatenate(
        [w[:, :gr0], w[:, gr0 + GLA_RANK:], w[:, gr0:gr0 + GLA_RANK],
         jnp.zeros((w.shape[0], IN_COLS_PADDED - w.shape[1]), w.dtype)], axis=1).astype(BF16)


def _prep_w_in(w_in):
    depth, d, n = w_in.shape
    tr = LANES
    return pl.pallas_call(
        _prep_w_in_kernel,
        grid=(depth, d // tr),
        in_specs=[pl.BlockSpec((1, tr, n), lambda l, i: (l, i, 0))],
        out_specs=pl.BlockSpec((1, tr, IN_COLS_PADDED), lambda l, i: (l, i, 0)),
        out_shape=jax.ShapeDtypeStruct((depth, d, IN_COLS_PADDED), BF16),
        compiler_params=_params("arbitrary", "arbitrary"),
        name="w_in_prep",
    )(w_in)


def _project_and_mix(x, mod, g, w_all, layer, wa2, ba, cos, sin, gla_tables, ret_tables):
    b, s, d = x.shape
    tm = TOKEN_TILE
    assert tm == SB_TQ and s % tm == 0 and tm % GLA_CHUNK == 0 and tm % RET_CHUNK == 0
    tok = lambda width: pl.BlockSpec((1, tm, width), lambda bi, i: (bi, i, 0))
    const = lambda a: pl.BlockSpec(a.shape, lambda bi, i: tuple(0 for _ in a.shape))
    f32 = lambda width: jax.ShapeDtypeStruct((b, s, width), F32)
    r = np.arange(SB_TK)
    tri = jnp.asarray(r[:, None] > r[None, :], BF16)
    consts = (wa2, ba)
    return pl.pallas_call(
        _proj_mix_kernel,
        grid=(b, s // tm),
        in_specs=[tok(d), pl.BlockSpec((1, 1, 6 * d), lambda bi, i: (bi, 0, 0)), const(g),
                  pl.BlockSpec((None,) + w_all.shape[1:], lambda bi, i: (layer, 0, 0))]
                 + [const(a) for a in consts]
                 + [pl.BlockSpec((tm, LANES), lambda bi, i: (i, 0))] * 2
                 + [const(a) for a in gla_tables + ret_tables + (tri,)],
        out_specs=[tok(GLA_V), tok(GLA_V), tok(RET_W), tok(RET_W), tok(SB_W)],
        out_shape=[f32(GLA_V), f32(GLA_V), f32(RET_W), f32(RET_W),
                   jax.ShapeDtypeStruct((b, s, SB_W), BF16)],
        scratch_shapes=[pltpu.VMEM((GLA_V, GLA_QK), F32), pltpu.VMEM((RET_W, RET_W), F32),
                        pltpu.VMEM((SB_W, s), BF16), pltpu.VMEM((s, SB_W), BF16),
                        pltpu.VMEM((SB_HEADS, tm, LANES), F32), pltpu.VMEM((SB_HEADS, tm, LANES), F32)],
        compiler_params=_params("arbitrary", "arbitrary"),
        name="project_and_mix",
    )(x, mod, g, w_all, *consts, cos, sin, *gla_tables, *ret_tables, tri)


def _gla_tables():
    c, nl = GLA_CHUNK, GLA_LEVELS
    t = np.arange(c)
    rows = []
    for l in range(nl):
        w = 1 << l
        start = (t // w) * w
        rows.append((t[None, :] >= start[:, None]) & (t[None, :] <= t[:, None]))
    for l in range(nl):
        w = 1 << l
        end = (t // w + 1) * w - 1
        rows.append((t[None, :] > t[:, None]) & (t[None, :] <= end[:, None]))
    rows.append(t[None, :] <= t[:, None])
    mcat = np.concatenate(rows, 0).astype(np.float32)
    mcat3 = np.concatenate([mcat] * 3, 1)
    masks = []
    for l in range(nl):
        w = 1 << l
        same = (t[:, None] // (2 * w)) == (t[None, :] // (2 * w))
        upper = ((t // w) % 2 == 1)[:, None]
        lower = ((t // w) % 2 == 0)[None, :]
        masks.append(np.tile(same & upper & lower, (1, GLA_HEADS)))
    masks.append(np.tile(np.eye(c, dtype=bool), (1, GLA_HEADS)))
    lmask = np.stack(masks).astype(np.float32)
    r = np.arange(GLA_HEADS * c)
    kmask = (r[:, None] // c == np.arange(GLA_QK)[None, :] // GLA_DK).astype(np.float32)
    vmask = (r[:, None] // c == np.arange(GLA_V)[None, :] // GLA_DV).astype(np.float32)
    smask = (np.arange(GLA_V)[:, None] // GLA_DV == np.arange(GLA_QK)[None, :] // GLA_DK).astype(np.float32)
    return (jnp.asarray(mcat3, BF16), jnp.asarray(lmask), jnp.asarray(kmask),
            jnp.asarray(vmask), jnp.asarray(smask))


def _split3(x):
    hi = x.astype(BF16)
    r1 = x - hi.astype(F32)
    mid = r1.astype(BF16)
    lo = (r1 - mid.astype(F32)).astype(BF16)
    return hi, mid, lo


def _gla_block(q_all, k_all, v_all, la_all, mcat_ref, lmask_ref, kmask_ref, vmask_ref, smask_ref, st_ref):
    c, nl = GLA_CHUNK, GLA_LEVELS
    nchunk = q_all.shape[0] // c
    kmask, vmask, smask = kmask_ref[...], vmask_ref[...], smask_ref[...]
    la3 = jnp.concatenate([jnp.concatenate(_split3(la_all[ci * c:(ci + 1) * c]), axis=0)
                           for ci in range(nchunk)], axis=1)
    cs_all = _dot(mcat_ref[...], la3)
    e_all = jnp.exp(cs_all)
    chunks = range(nchunk)
    q = [q_all[ci * c:(ci + 1) * c] for ci in chunks]
    k = [k_all[ci * c:(ci + 1) * c] for ci in chunks]
    v = [v_all[ci * c:(ci + 1) * c] for ci in chunks]
    cs = [cs_all[:, ci * GLA_QK:(ci + 1) * GLA_QK] for ci in chunks]
    e = [e_all[:, ci * GLA_QK:(ci + 1) * GLA_QK] for ci in chunks]
    scores = [jnp.zeros((c, GLA_HEADS * c), F32) for _ in chunks]
    for l in range(nl + 1):
        for ci in chunks:
            if l < nl:
                qt = q[ci] * e[ci][l * c:(l + 1) * c]
                kt = k[ci] * e[ci][(nl + l) * c:(nl + l + 1) * c]
            else:
                qt, kt = q[ci], k[ci]
            krows = (jnp.concatenate([kt] * GLA_HEADS, axis=0) * kmask).astype(BF16)
            scores[ci] = scores[ci] + _dot_nt(qt.astype(BF16), krows) * lmask_ref[l]
    intra = [_dot(scores[ci].astype(BF16), (jnp.concatenate([v[ci]] * GLA_HEADS, axis=0) * vmask).astype(BF16))
             for ci in chunks]
    bcum = [cs[ci][2 * nl * c:(2 * nl + 1) * c] for ci in chunks]
    qd = [(q[ci] * e[ci][2 * nl * c:(2 * nl + 1) * c]).astype(BF16) for ci in chunks]
    upd = [_dot_tn(v[ci].astype(BF16), (k[ci] * jnp.exp(bcum[ci][c - 1:c, :] - bcum[ci])).astype(BF16)) * smask
           for ci in chunks]
    outs = []
    st = st_ref[...]
    for ci in chunks:
        outs.append(intra[ci] + _dot_nt(qd[ci], st.astype(BF16)))
        st = st * jnp.exp(bcum[ci][c - 1:c, :]) + upd[ci]
    st_ref[...] = st
    return jnp.concatenate(outs, axis=0)


def _ret_tables():
    c = RET_CHUNK
    gam = 1.0 - 2.0 ** (-5.0 - np.arange(RET_HEADS, dtype=np.float64))
    t = np.arange(c)
    rel = t[:, None] - t[None, :]
    dmat = np.where(rel[None] >= 0, gam[:, None, None] ** np.maximum(rel, 0)[None], 0.0)
    dall = np.concatenate(list(dmat), axis=1)
    xi = np.repeat((gam[None, :] ** (t[:, None] + 1.0)), RET_DK, axis=1)
    zeta = np.repeat((gam[None, :] ** (c - 1.0 - t[:, None])), RET_DK, axis=1)
    gc = np.repeat(gam ** c, RET_DK)[:, None] * np.ones((1, RET_W))
    r = np.arange(RET_HEADS * c)
    hmask = (r[:, None] // c == np.arange(RET_W)[None, :] // RET_DK)
    smask = (np.arange(RET_W)[:, None] // RET_DK == np.arange(RET_W)[None, :] // RET_DV)
    return tuple(jnp.asarray(a, F32) for a in (dall, xi, zeta, gc, hmask, smask))


def _ret_chunk(q, k, v, dall_ref, xi_ref, zeta_ref, gc_ref, hmask_ref, smask_ref, r_ref):
    hmask = hmask_ref[...]
    krows = (jnp.concatenate([k] * RET_HEADS, axis=0) * hmask).astype(BF16)
    scores = _dot_nt(q.astype(BF16), krows) * dall_ref[...]
    vbd = (jnp.concatenate([v] * RET_HEADS, axis=0) * hmask).astype(BF16)
    o = _dot(scores.astype(BF16), vbd)
    r = r_ref[...]
    o = o + _dot((q * xi_ref[...]).astype(BF16), r.astype(BF16))
    upd = _dot_tn((k * zeta_ref[...]).astype(BF16), v.astype(BF16))
    r_ref[...] = r * gc_ref[...] + upd * smask_ref[...]
    return o


def _sb_block(i, q, kT_new, v_new, kT_ref, v_ref, w_ref, acc_ref, c_ref, interleave):
    tq, tk = SB_TQ, SB_TK
    npairs = SB_HEADS // 2
    lane = lax.broadcasted_iota(jnp.int32, (tq, LANES), 1)
    qs = []
    for pr in range(npairs):
        q2 = q[:, pr * LANES:(pr + 1) * LANES]
        zero = jnp.zeros_like(q2)
        qs += [jnp.where(lane < SB_DK, q2, zero), jnp.where(lane >= SB_DK, q2, zero)]
    w = w_ref[...]

    def softplus2(t):
        return jnp.maximum(jnp.log(1.0 + jnp.exp2(jnp.minimum(t, SB_EXP_CLAMP))) * _log2e(t.shape[1]), t)

    def masked_softplus(zs):
        hq = tk // 2
        below = (lax.broadcasted_iota(jnp.int32, (hq, hq), 1) < lax.broadcasted_iota(jnp.int32, (hq, hq), 0))
        z_ul, z_ll, z_lr = zs[:hq, :hq], zs[hq:, :hq], zs[hq:, hq:]
        sp_ul = jnp.where(below, softplus2(z_ul), 0.0)
        sp_ll = softplus2(z_ll)
        sp_lr = jnp.where(below, softplus2(z_lr), 0.0)
        hidden = jnp.zeros((hq, hq), F32)
        sp = jnp.concatenate([jnp.concatenate([sp_ul, hidden], axis=1),
                              jnp.concatenate([sp_ll, sp_lr], axis=1)], axis=0)

        def p_of(t):
            p_ul = jnp.where(below, jnp.exp2((z_ul - sp_ul) - t[:hq, :hq]), 0.0)
            p_ll = jnp.exp2((z_ll - sp_ll) - t[hq:, :hq])
            p_lr = jnp.where(below, jnp.exp2((z_lr - sp_lr) - t[hq:, hq:]), 0.0)
            return jnp.concatenate([jnp.concatenate([p_ul, hidden], axis=1),
                                    jnp.concatenate([p_ll, p_lr], axis=1)], axis=0)

        return sp, p_of

    def tile(j, r0, r1, mask_rows, local=None, valid=None, heads=range(SB_HEADS)):
        assert mask_rows in (0, tk)
        start = pl.multiple_of(j * tk, tk)
        sections = [(r0, r0 + mask_rows, True), (r0 + mask_rows, r1, False)]
        for pr in sorted({hd // 2 for hd in heads}):
            cols = slice(pr * LANES, (pr + 1) * LANES)
            if local is None:
                kT2 = kT_ref[cols, pl.ds(start, tk)]
                v2 = v_ref[pl.ds(start, tk), cols]
            else:
                kT2 = kT_new[cols, local * tk:(local + 1) * tk]
                v2 = v_new[local * tk:(local + 1) * tk, cols]
            if valid is not None:
                v2 = jnp.where(valid, v2, jnp.zeros_like(v2))
            for hd in (h for h in (2 * pr, 2 * pr + 1) if h in heads):
                z = _dot(qs[hd][r0:r1], kT2)
                for a, b, masked in sections:
                    if a == b:
                        continue
                    zs = z[a - r0:b - r0]
                    if masked:
                        sp, p_of = masked_softplus(zs)
                        r = _dot(sp.astype(BF16), w)
                        c_ref[hd, a:b, :] = jnp.broadcast_to(r[:, 0:1] + sp[:, 0:1], (b - a, LANES))
                        acc_ref[hd, a:b, :] = _dot(p_of(r).astype(BF16), v2)
                        continue
                    c = c_ref[hd, a:b, :]
                    sp = softplus2(zs)
                    r = _dot(sp.astype(BF16), w)
                    p = jnp.exp2((zs - sp) - (r + jnp.concatenate([c] * (tk // LANES), axis=1)))
                    total = r[:, 0:1] + sp[:, 0:1]
                    if valid is not None:
                        total = jnp.where(valid, total, 0.0)
                    c_ref[hd, a:b, :] = c + jnp.broadcast_to(total, c.shape)
                    acc_ref[hd, a:b, :] += _dot(p.astype(BF16), v2)

    nk = tq // tk
    half = tq // 2
    fillers = list(interleave)
    for hd in range(SB_HEADS):
        for d in reversed(range(nk)):
            tile(i * nk + d, d * tk, tq, tk, local=d, heads=(hd,))
            if fillers:
                fillers.pop(0)()
        tile(jnp.maximum(i * nk - 1, 0), 0, half, 0, valid=i > 0, heads=(hd,))
        if fillers:
            fillers.pop(0)()
    assert not fillers

    def mins():
        c = c_ref[...]
        return jnp.min(c[:, :half]), jnp.min(c[:, half:])

    def cond(st):
        return jnp.logical_and(st[0] < i * nk, jnp.minimum(st[1], st[2]) < SB_UNDERFLOW_LOG2)

    def body(st):
        j = i * nk - 1 - st[0]
        lower_done = st[2] >= SB_UNDERFLOW_LOG2

        def first():
            @pl.when(jnp.logical_not(lower_done))
            def _():
                tile(j, half, tq, 0)

        def later():
            lax.cond(lower_done, lambda: tile(j, 0, half, 0), lambda: tile(j, 0, tq, 0))

        lax.cond(st[0] == 0, first, later)
        return (st[0] + 1,) + mins()

    lax.while_loop(cond, body, (jnp.int32(0),) + mins())
    return jnp.concatenate(
        [jnp.where(lane < SB_DK, acc_ref[2 * pr], acc_ref[2 * pr + 1]).astype(BF16) for pr in range(npairs)], axis=1)


def _head_norm_gate(o, g, gate, bd):
    ms = _dot((o * o).astype(BF16), bd)
    return o * lax.rsqrt(ms + EPS) * g * (gate * _sigmoid(gate))


def _ffn_kernel(x_ref, mod_ref, og_ref, gg_ref, or_ref, rg_ref, osb_ref, gng_ref, rng_ref, bd_ref,
                wout_ref, n2g_ref, wg_ref, wu_ref, wd_ref, fg_ref, o_ref, p_ref, *, final):
    d = D_MODEL
    x = x_ref[0]
    mod = mod_ref[0]
    g1, sh2, sc2, g2 = mod[:, 2 * d:3 * d], mod[:, 3 * d:4 * d], mod[:, 4 * d:5 * d], mod[:, 5 * d:6 * d]
    bd = bd_ref[...]
    o_gla = _head_norm_gate(og_ref[0], gng_ref[...], gg_ref[0], bd)
    o_ret = _head_norm_gate(or_ref[0], rng_ref[...], rg_ref[0], bd)
    o = jnp.concatenate([o_gla.astype(BF16), o_ret.astype(BF16), osb_ref[0]], axis=1)
    x1 = x + g1 * _dot(o, wout_ref[...])
    ms = jnp.mean(x1 * x1, axis=-1, keepdims=True)
    h = (x1 * lax.rsqrt(ms + EPS) * (n2g_ref[...] * (1.0 + sc2)) + sh2).astype(BF16)
    for c0 in range(0, D_FF, FF_CHUNK):
        a = _dot(h, wg_ref[:, c0:c0 + FF_CHUNK])
        u = _dot(h, wu_ref[:, c0:c0 + FF_CHUNK])
        p_ref[:, c0:c0 + FF_CHUNK] = (a * _sigmoid(a) * u).astype(BF16)
    x2 = x1 + g2 * _dot(p_ref[...], wd_ref[...])
    if final:
        ms = jnp.mean(x2 * x2, axis=-1, keepdims=True)
        x2 = x2 * lax.rsqrt(ms + EPS) * fg_ref[...]
    o_ref[0] = x2


def _out_ffn(x, mod, og, gg, orr, rg, osb, gng, rng, bd, wout, n2g, wg, wu, wd, fg, layer, final):
    b, s, d = x.shape
    tm = min(TOKEN_TILE, s)
    tok = lambda width: pl.BlockSpec((1, tm, width), lambda bi, i: (bi, i, 0))
    const = lambda a: pl.BlockSpec(a.shape, lambda bi, i: tuple(0 for _ in a.shape),
                                   pipeline_mode=pl.Buffered(1))
    weight = lambda a: pl.BlockSpec((None,) + a.shape[1:], lambda bi, i: (layer, 0, 0),
                                    pipeline_mode=pl.Buffered(1))
    return pl.pallas_call(
        functools.partial(_ffn_kernel, final=final),
        grid=(b, s // tm),
        in_specs=[tok(d), pl.BlockSpec((1, 1, 6 * d), lambda bi, i: (bi, 0, 0)),
                  tok(GLA_V), tok(GLA_V), tok(RET_W), tok(RET_W), tok(SB_W),
                  const(gng), const(rng), const(bd), weight(wout), const(n2g),
                  weight(wg), weight(wu), weight(wd), const(fg)],
        out_specs=tok(d),
        out_shape=jax.ShapeDtypeStruct((b, s, d), F32),
        scratch_shapes=[pltpu.VMEM((tm, D_FF), BF16)],
        compiler_params=_params("arbitrary", "arbitrary"),
        name="out_projection_ffn",
    )(x, mod, og, gg, orr, rg, osb, gng, rng, bd, wout, n2g, wg, wu, wd, fg)


def _rope_tables(s):
    inv = ROPE_BASE ** (-np.arange(0, RET_DK, 2, dtype=np.float64) / RET_DK)
    ang = np.arange(s, dtype=np.float64)[:, None] * inv[None, :]
    cos = np.repeat(np.cos(ang), 2, axis=1)
    sin = np.repeat(np.sin(ang), 2, axis=1) * np.tile([-1.0, 1.0], RET_DK // 2)[None, :]
    reps = LANES // RET_DK
    return jnp.asarray(np.tile(cos, (1, reps)), F32), jnp.asarray(np.tile(sin, (1, reps)), F32)


def kernel(x, c, ada_w, ada_b, norm1_g, norm2_g, w_in, gla_wa2, gla_ba, gla_norm_g, ret_norm_g,
           w_out, ffn_wg, ffn_wu, ffn_wd, final_g):
    b, s, d = x.shape
    mods = _modulation(c, ada_w, ada_b)
    cos, sin = _rope_tables(s)
    gla_tables = _gla_tables()
    ret_tables = _ret_tables()
    hd = np.arange(GLA_V)
    bd = jnp.asarray((hd[:, None] // GLA_DV == hd[None, :] // GLA_DV) / float(GLA_DV), BF16)
    w_in_b = _prep_w_in(w_in)
    w_out_b, wg_b, wu_b, wd_b = (a.astype(BF16) for a in (w_out, ffn_wg, ffn_wu, ffn_wd))
    for l in range(DEPTH):
        wa2 = jnp.concatenate([gla_wa2[l], jnp.zeros((LANES - GLA_RANK, GLA_QK), F32)], axis=0).astype(BF16)
        mod = mods[l].reshape(b, 1, 6 * d)
        o_gla, gg, o_ret, rg, o_sb = _project_and_mix(
            x, mod, norm1_g[l].reshape(1, d), w_in_b, l, wa2, gla_ba[l].reshape(1, GLA_QK), cos, sin,
            gla_tables, ret_tables)
        x = _out_ffn(x, mod, o_gla, gg, o_ret, rg, o_sb,
                     jnp.tile(gla_norm_g[l], GLA_HEADS).reshape(1, GLA_V),
                     jnp.tile(ret_norm_g[l], RET_HEADS).reshape(1, RET_W), bd,
                     w_out_b, norm2_g[l].reshape(1, d), wg_b, wu_b, wd_b,
                     final_g.reshape(1, d), l, l == DEPTH - 1)
    return x
```
